```python
import math
import jax, jax.numpy as jnp
from jax import lax
import numpy as np

D_MODEL = 1024
BATCH = 4
SEQ = 8192
DEPTH = 1

GLA_HEADS = 4
GLA_DV = D_MODEL // GLA_HEADS
GLA_DK = GLA_DV // 2
GLA_RANK = 16
GLA_TAU = 16.0
GLA_CHUNK = 64
SWA_HD = 64
SWA_HEADS = D_MODEL // SWA_HD
SWA_KV_HEADS = 2
SWA_WINDOW = 128
ROPE_DIM = SWA_HD // 4
ROPE_THETA = 500000.0
D_FF = 2816
CONV_W = 3
N_BRANCH = 2
LN_EPS = 1e-5
RMS_EPS = 1e-6
ALPHA = (2.0 * DEPTH) ** 0.25
BETA = (8.0 * DEPTH) ** -0.25
N_MOD = 6

SPLITS = (
    GLA_HEADS * GLA_DK,
    GLA_HEADS * GLA_DK,
    GLA_HEADS * GLA_DV,
    GLA_HEADS * GLA_DV,
    GLA_RANK,
    SWA_HEADS * SWA_HD,
    SWA_KV_HEADS * SWA_HD,
    SWA_KV_HEADS * SWA_HD,
    N_BRANCH * D_MODEL,
)
D_IN = sum(SPLITS)
SPLIT_POINTS = tuple(int(v) for v in np.cumsum(SPLITS)[:-1])

kernel_name = "hybrid_gla_swa_convffn_deepnorm_adaln"


def layer_norm(x, eps=LN_EPS):
    xf = x.astype(jnp.float32)
    mu = jnp.mean(xf, -1, keepdims=True)
    var = jnp.mean(jnp.square(xf - mu), -1, keepdims=True)
    return ((xf - mu) * lax.rsqrt(var + eps)).astype(x.dtype)


def layer_norm_affine(x, g, b):
    return layer_norm(x) * g + b


def rms_norm(x, g, eps=RMS_EPS):
    xf = x.astype(jnp.float32)
    y = xf * lax.rsqrt(jnp.mean(jnp.square(xf), -1, keepdims=True) + eps)
    return y.astype(x.dtype) * g


def partial_rope(t, cos, sin):
    half = ROPE_DIM // 2
    t1, t2, rest = t[..., :half], t[..., half:ROPE_DIM], t[..., ROPE_DIM:]
    return jnp.concatenate([t1 * cos - t2 * sin, t2 * cos + t1 * sin, rest], axis=-1)


def gla_chunked(q, k, v, log_a):
    B, S, H, DK = q.shape
    DV = v.shape[-1]
    n = S // GLA_CHUNK

    def to_chunks(t):
        return t.reshape(B, n, GLA_CHUNK, H, t.shape[-1]).transpose(1, 0, 3, 2, 4)

    qc, kc, vc, gc = map(to_chunks, (q, k, v, log_a))
    gcum = jnp.cumsum(gc.astype(jnp.float32), axis=3)
    g_last = gcum[..., -1:, :]
    g_mid = gcum[..., GLA_CHUNK // 2 - 1:GLA_CHUNK // 2, :]
    q_mid = qc * jnp.exp(gcum - g_mid)
    k_mid = kc * jnp.exp(g_mid - gcum)
    a_intra = jnp.einsum('nbhik,nbhjk->nbhij', q_mid, k_mid)
    causal = jnp.tril(jnp.ones((GLA_CHUNK, GLA_CHUNK), dtype=bool))
    a_intra = jnp.where(causal, a_intra, 0.0)
    o_intra = jnp.einsum('nbhij,nbhjv->nbhiv', a_intra, vc.astype(jnp.float32))
    q_in = qc * jnp.exp(gcum)
    k_out = kc * jnp.exp(g_last - gcum)
    decay = jnp.exp(g_last)

    def step(state, inp):
        q_i, k_o, v_i, dec = inp
        o = jnp.einsum('bhik,bhkv->bhiv', q_i, state)
        state = state * dec[:, :, 0, :, None] + jnp.einsum('bhjk,bhjv->bhkv', k_o, v_i.astype(jnp.float32))
        return state, o

    state0 = jnp.zeros((B, H, DK, DV), jnp.float32)
    _, o_inter = lax.scan(step, state0, (q_in, k_out, vc, decay))
    o = (o_intra + o_inter).astype(v.dtype)
    return o.transpose(1, 0, 3, 2, 4).reshape(B, S, H, DV)


def swa_with_sinks(q, k, v, sinks):
    B, S, HQ, hd = q.shape
    HKV = k.shape[2]
    G = HQ // HKV
    W = SWA_WINDOW
    n = S // W
    qb = q.reshape(B, n, W, HKV, G, hd)

    def band(t):
        tb = t.reshape(B, n, W, HKV, hd)
        prev = jnp.pad(tb[:, :-1], ((0, 0), (1, 0), (0, 0), (0, 0), (0, 0)))
        return jnp.concatenate([prev, tb], axis=2)

    kb, vb = band(k), band(v)
    s = jnp.einsum('bnqkgd,bnjkd->bkgnqj', qb, kb).astype(jnp.float32) * (hd ** -0.5)
    qi = jnp.arange(W)[:, None]
    kj = jnp.arange(2 * W)[None, :]
    diff = W + qi - kj
    win = (diff >= 0) & (diff < W)
    blk = jnp.arange(n)[:, None, None]
    mask = win[None] & ((blk > 0) | (kj >= W)[None])
    s = jnp.where(mask[None, None, None], s, -jnp.inf)
    sink = sinks.astype(jnp.float32).reshape(HKV, G)[None, :, :, None, None, None]
    m = jnp.maximum(jnp.max(s, -1, keepdims=True), sink)
    p = jnp.exp(s - m)
    p = p / (jnp.sum(p, -1, keepdims=True) + jnp.exp(sink - m))
    o = jnp.einsum('bkgnqj,bnjkd->bnqkgd', p.astype(v.dtype), vb)
    return o.reshape(B, S, HQ, hd)


def setup_inputs(seed: int = 0) -> dict:
    key = jax.random.key(seed)
    ks = jax.random.split(key, 20)
    f32 = jnp.float32
    L, D = DEPTH, D_MODEL
    nrm = lambda k, shape, s: jax.random.normal(k, shape, f32) * s
    return {
        "x": nrm(ks[0], (BATCH, SEQ, D), 1.0),
        "c": nrm(ks[1], (BATCH, D), 1.0),
        "positions": jnp.broadcast_to(jnp.arange(SEQ, dtype=jnp.int32), (BATCH, SEQ)),
        "w_ada": nrm(ks[2], (L, D, N_MOD * D), 0.5 * D ** -0.5),
        "b_ada": nrm(ks[3], (L, N_MOD * D), 0.01),
        "w_in": nrm(ks[4], (L, D, D_IN), D ** -0.5),
        "gla_w_lr": nrm(ks[5], (L, GLA_RANK, GLA_HEADS * GLA_DK), GLA_RANK ** -0.5),
        "gla_b_lr": nrm(ks[6], (L, GLA_HEADS * GLA_DK), 0.01),
        "gla_norm_g": 1.0 + nrm(ks[7], (L, GLA_DV), 0.01),
        "swa_sinks": nrm(ks[8], (L, SWA_HEADS), 1.0),
        "w_o": nrm(ks[9], (L, D, D), BETA * D ** -0.5),
        "ln1_g": 1.0 + nrm(ks[10], (L, D), 0.01),
        "ln1_b": nrm(ks[11], (L, D), 0.01),
        "w_up": nrm(ks[12], (L, D, 2 * D_FF), D ** -0.5),
        "conv_w": nrm(ks[13], (L, CONV_W, 2 * D_FF), CONV_W ** -0.5),
        "conv_b": nrm(ks[14], (L, 2 * D_FF), 0.01),
        "w_down": nrm(ks[15], (L, D_FF, D), BETA * D_FF ** -0.5),
        "ln2_g": 1.0 + nrm(ks[16], (L, D), 0.01),
        "ln2_b": nrm(ks[17], (L, D), 0.01),
    }


def reference(x, c, positions, w_ada, b_ada, w_in, gla_w_lr, gla_b_lr, gla_norm_g, swa_sinks,
              w_o, ln1_g, ln1_b, w_up, conv_w, conv_b, w_down, ln2_g, ln2_b):
    B, S, D = x.shape
    inv_freq = ROPE_THETA ** (-(jnp.arange(0, ROPE_DIM, 2, dtype=jnp.float32) / ROPE_DIM))
    ang = positions.astype(jnp.float32)[..., None] * inv_freq
    cos = jnp.cos(ang)[:, :, None, :].astype(x.dtype)
    sin = jnp.sin(ang)[:, :, None, :].astype(x.dtype)
    c_act = jax.nn.silu(c)

    for l in range(DEPTH):
        mod = c_act @ w_ada[l] + b_ada[l]
        shift1, scale1, gate1, shift2, scale2, gate2 = [m[:, None, :] for m in jnp.split(mod, N_MOD, axis=-1)]

        h = layer_norm(x) * (1.0 + scale1) + shift1
        proj = h @ w_in[l]
        qa, ka, va, ra, lra, qb, kb, vb, gates = jnp.split(proj, SPLIT_POINTS, axis=-1)

        log_a = jax.nn.log_sigmoid((lra @ gla_w_lr[l] + gla_b_lr[l]).astype(jnp.float32)) / GLA_TAU
        qa = qa.reshape(B, S, GLA_HEADS, GLA_DK) * (GLA_DK ** -0.5)
        ka = ka.reshape(B, S, GLA_HEADS, GLA_DK)
        va = va.reshape(B, S, GLA_HEADS, GLA_DV)
        log_a = log_a.reshape(B, S, GLA_HEADS, GLA_DK)
        o_a = gla_chunked(qa, ka, va, log_a)
        y_a = (rms_norm(o_a, gla_norm_g[l]) * jax.nn.silu(ra.reshape(B, S, GLA_HEADS, GLA_DV))).reshape(B, S, D)

        qb = partial_rope(qb.reshape(B, S, SWA_HEADS, SWA_HD), cos, sin)
        kb = partial_rope(kb.reshape(B, S, SWA_KV_HEADS, SWA_HD), cos, sin)
        vb = vb.reshape(B, S, SWA_KV_HEADS, SWA_HD)
        y_b = swa_with_sinks(qb, kb, vb, swa_sinks[l]).reshape(B, S, D)

        g_a, g_b = jnp.split(gates, N_BRANCH, axis=-1)
        y = jax.nn.sigmoid(g_a) * y_a + jax.nn.sigmoid(g_b) * y_b
        x = layer_norm_affine(ALPHA * x + gate1 * (y @ w_o[l]), ln1_g[l], ln1_b[l])

        h2 = layer_norm(x) * (1.0 + scale2) + shift2
        u = h2 @ w_up[l]
        up = jnp.pad(u, ((0, 0), (CONV_W - 1, 0), (0, 0)))
        u = conv_b[l] + sum(conv_w[l, t] * up[:, t:t + S] for t in range(CONV_W))
        u_gate, u_val = jnp.split(u, 2, axis=-1)
        f = jax.nn.gelu(u_gate, approximate=False) * u_val
        x = layer_norm_affine(ALPHA * x + gate2 * (f @ w_down[l]), ln2_g[l], ln2_b[l])

    return x
```

```python
import functools

import jax
import jax.numpy as jnp
from jax import lax
from jax.experimental import pallas as pl
from jax.experimental.pallas import tpu as pltpu

LANES = 128
SUBLANES = 8
VMEM_LIMIT_BYTES = 56 * 1024 * 1024

GLA_HEADS = 4
GLA_RANK = 16
GLA_TAU = 16.0
GLA_CHUNK = 64
SWA_HD = 64
SWA_KV_HEADS = 2
SWA_WINDOW = 128
ROPE_DIM = SWA_HD // 4
ROPE_THETA = 500000.0
CONV_W = 3
LN_EPS = 1e-5
RMS_EPS = 1e-6
N_MOD = 6

SEQ_TILE = 256
MOD_COL_TILE = 1536

BF16 = jnp.bfloat16
F32 = jnp.float32


def _dot(a, b):
    return jnp.dot(a, b, preferred_element_type=F32)


def _dot_nt(a, b):
    return lax.dot_general(a, b, (((1,), (1,)), ((), ())), preferred_element_type=F32)


def _dot_tn(a, b):
    return lax.dot_general(a, b, (((0,), (0,)), ((), ())), preferred_element_type=F32)


def _layer_norm(x):
    mu = jnp.mean(x, axis=-1, keepdims=True)
    xc = x - mu
    var = jnp.mean(xc * xc, axis=-1, keepdims=True)
    return xc * lax.rsqrt(var + LN_EPS)


def _split3_bf16(a):
    hi = a.astype(BF16)
    r1 = a - hi.astype(F32)
    mid = r1.astype(BF16)
    lo = (r1 - mid.astype(F32)).astype(BF16)
    return hi, mid, lo


def _mod_kernel(c_ref, w_ref, b_ref, o_ref):
    c = c_ref[...]
    c_act = c * jax.nn.sigmoid(c)
    o_ref[...] = _dot(c_act.astype(BF16), w_ref[...].astype(BF16)) + b_ref[...]


def _modulation(c, w_ada, b_ada):
    B, D = c.shape
    N = w_ada.shape[1]
    return pl.pallas_call(
        _mod_kernel,
        grid=(N // MOD_COL_TILE,),
        in_specs=[
            pl.BlockSpec((B, D), lambda j: (0, 0)),
            pl.BlockSpec((D, MOD_COL_TILE), lambda j: (0, j)),
            pl.BlockSpec((1, MOD_COL_TILE), lambda j: (0, j)),
        ],
        out_specs=pl.BlockSpec((B, MOD_COL_TILE), lambda j: (0, j)),
        out_shape=jax.ShapeDtypeStruct((B, N), F32),
        compiler_params=pltpu.CompilerParams(
            dimension_semantics=("arbitrary",), vmem_limit_bytes=VMEM_LIMIT_BYTES),
        name="adaln_mod",
    )(c, w_ada, b_ada.reshape(1, N))


def _mixer_kernel(x_ref, pos_ref, mod_ref, invf_ref, sinks_ref,
                  wqa_ref, wka_ref, wva_ref, wra_ref, wlra_ref, wlr_ref, blr_ref, gnorm_ref,
                  wqb_ref, wkvb_ref, wg_ref, wo_ref, ln1g_ref, ln1b_ref,
                  o_ref,
                  state_ref, kf_ref, vf_ref, ya_ref, yb_ref,
                  *, alpha):
    TS, D = x_ref.shape[1], x_ref.shape[2]
    DK = wqa_ref.shape[1] // GLA_HEADS
    DV = wva_ref.shape[1] // GLA_HEADS
    W = SWA_WINDOW
    C = GLA_CHUNK
    s_idx = pl.program_id(1)

    @pl.when(s_idx == 0)
    def _():
        state_ref[...] = jnp.zeros_like(state_ref)
        kf_ref[:, 0:W, :] = jnp.zeros((4, W, LANES), BF16)
        vf_ref[:, 0:W, :] = jnp.zeros((2, W, LANES), BF16)

    x = x_ref[0]
    mod = mod_ref[0]
    shift1, scale1, gate1 = mod[0:1], mod[1:2], mod[2:3]
    h = (_layer_norm(x) * (1.0 + scale1) + shift1).astype(BF16)

    lra = _dot(h, wlra_ref[...]).astype(BF16)
    z = _dot(lra, wlr_ref[...]) + blr_ref[...]
    log_a = (jnp.minimum(z, 0.0) - jnp.log1p(jnp.exp(-jnp.abs(z)))) * (1.0 / GLA_TAU)
    qa = _dot(h, wqa_ref[...]) * (DK ** -0.5)
    ka = _dot(h, wka_ref[...])
    va = _dot(h, wva_ref[...]).astype(BF16)
    ra = _dot(h, wra_ref[...])

    ri = lax.broadcasted_iota(jnp.int32, (C, C), 0)
    ci = lax.broadcasted_iota(jnp.int32, (C, C), 1)
    causal = ci <= ri
    tri = jnp.where(causal, 1.0, 0.0).astype(BF16)
    gnorm = gnorm_ref[...]

    for c in range(TS // C):
        rows = slice(c * C, (c + 1) * C)
        hi, mid, lo = _split3_bf16(log_a[rows])
        gcum = _dot(tri, hi) + _dot(tri, mid) + _dot(tri, lo)
        g_last = gcum[C - 1:C]
        g_mid = gcum[C // 2 - 1:C // 2]
        q_c, k_c = qa[rows], ka[rows]
        q_mid = (q_c * jnp.exp(gcum - g_mid)).astype(BF16)
        k_mid = (k_c * jnp.exp(g_mid - gcum)).astype(BF16)
        q_in = (q_c * jnp.exp(gcum)).astype(BF16)
        k_out = (k_c * jnp.exp(g_last - gcum)).astype(BF16)
        decay = jnp.exp(g_last)
        for hh in range(GLA_HEADS):
            kc = slice(hh * DK, (hh + 1) * DK)
            vc = slice(hh * DV, (hh + 1) * DV)
            a_intra = jnp.where(causal, _dot_nt(q_mid[:, kc], k_mid[:, kc]), 0.0)
            v_h = va[rows, vc]
            st = state_ref[hh]
            o = _dot(a_intra.astype(BF16), v_h) + _dot_nt(q_in[:, kc], st.astype(BF16))
            state_ref[hh] = st * decay[:, kc] + _dot_tn(v_h, k_out[:, kc])
            rms = lax.rsqrt(jnp.mean(o * o, axis=-1, keepdims=True) + RMS_EPS)
            r_h = ra[rows, vc]
            ya_ref[rows, vc] = (o * rms) * gnorm * (r_h * jax.nn.sigmoid(r_h))

    pos_row = pos_ref[0].astype(F32)
    posb = jnp.concatenate(
        [jnp.broadcast_to(pos_row[:, j * LANES:(j + 1) * LANES], (LANES, LANES)).T
         for j in range(TS // LANES)], axis=0)
    ang = posb * invf_ref[...]
    cos_t, sin_t = jnp.cos(ang), jnp.sin(ang)
    lane = lax.broadcasted_iota(jnp.int32, (1, LANES), 1)
    lane_in_head = lane % SWA_HD
    half = ROPE_DIM // 2
    co_dn = jnp.where(lane_in_head < half, -sin_t, 0.0)
    co_up = jnp.where((lane_in_head >= half) & (lane_in_head < ROPE_DIM), sin_t, 0.0)

    def rope(t):
        return t * cos_t + pltpu.roll(t, LANES - half, 1) * co_dn + pltpu.roll(t, half, 1) * co_up

    kvb = _dot(h, wkvb_ref[...])
    k_rot = rope(kvb[:, 0:LANES])
    v_b = kvb[:, LANES:2 * LANES]
    k_sw = pltpu.roll(k_rot, SWA_HD, 1)
    v_sw = pltpu.roll(v_b, SWA_HD, 1)
    lo_half = lane < SWA_HD
    cur = slice(W, W + TS)
    kf_ref[0, cur, :] = jnp.where(lo_half, k_rot, 0.0).astype(BF16)
    kf_ref[1, cur, :] = jnp.where(lo_half, 0.0, k_sw).astype(BF16)
    kf_ref[2, cur, :] = jnp.where(lo_half, k_sw, 0.0).astype(BF16)
    kf_ref[3, cur, :] = jnp.where(lo_half, 0.0, k_rot).astype(BF16)
    vf_ref[0, cur, :] = jnp.where(lo_half, v_b, v_sw).astype(BF16)
    vf_ref[1, cur, :] = jnp.where(lo_half, v_sw, v_b).astype(BF16)

    qb = _dot(h, wqb_ref[...]) * (SWA_HD ** -0.5)
    qi = lax.broadcasted_iota(jnp.int32, (W, 2 * W), 0)
    kj = lax.broadcasted_iota(jnp.int32, (W, 2 * W), 1)
    win = (kj > qi) & (kj <= qi + W)
    groups_per_kv = (wqb_ref.shape[1] // LANES) // SWA_KV_HEADS
    for g in range(SWA_KV_HEADS):
        for p in range(groups_per_kv):
            grp = g * groups_per_kv + p
            cols = slice(grp * LANES, (grp + 1) * LANES)
            q_rot = rope(qb[:, cols]).astype(BF16)
            for j in range(TS // W):
                if j == 0:
                    mask = win & ((kj >= W) | (s_idx > 0))
                else:
                    mask = win
                band = slice(j * W, j * W + 2 * W)
                qrows = slice(j * W, (j + 1) * W)
                v_dup = vf_ref[g, band, :]
                outs = []
                for e in range(2):
                    sink = sinks_ref[2 * grp + e]
                    sc = jnp.where(mask, _dot_nt(q_rot[qrows], kf_ref[2 * g + e, band, :]), -jnp.inf)
                    m = jnp.maximum(jnp.max(sc, axis=-1, keepdims=True), sink)
                    pexp = jnp.exp(sc - m)
                    denom = jnp.sum(pexp, axis=-1, keepdims=True) + jnp.exp(sink - m)
                    outs.append(_dot(pexp.astype(BF16), v_dup) * (1.0 / denom))
                yb_ref[qrows, cols] = jnp.where(lo_half, outs[0], outs[1])
    kf_ref[:, 0:W, :] = kf_ref[:, TS:TS + W, :]
    vf_ref[:, 0:W, :] = vf_ref[:, TS:TS + W, :]

    gates = _dot(h, wg_ref[...])
    y = jax.nn.sigmoid(gates[:, 0:D]) * ya_ref[...] + jax.nn.sigmoid(gates[:, D:2 * D]) * yb_ref[...]
    t = _dot(y.astype(BF16), wo_ref[...])
    o_ref[0] = _layer_norm(alpha * x + gate1 * t) * ln1g_ref[...] + ln1b_ref[...]


def _const_spec(shape):
    nd = len(shape)
    return pl.BlockSpec(shape, lambda b, s: (0,) * nd)


def _token_mixer(x, positions, mod, weights, *, alpha):
    B, S, D = x.shape
    TS = SEQ_TILE
    (invf, sinks, wqa, wka, wva, wra, wlra, wlr, blr, gnorm, wqb, wkvb, wg, wo, ln1g, ln1b) = weights
    vmem_weights = (wqa, wka, wva, wra, wlra, wlr, blr, gnorm, wqb, wkvb, wg, wo, ln1g, ln1b)
    DK = wqa.shape[1] // GLA_HEADS
    DV = wva.shape[1] // GLA_HEADS
    in_specs = [
        pl.BlockSpec((1, TS, D), lambda b, s: (b, s, 0)),
        pl.BlockSpec((1, 1, TS), lambda b, s: (b, 0, s)),
        pl.BlockSpec((1, N_MOD, D), lambda b, s: (b, 0, 0)),
        _const_spec(invf.shape),
        pl.BlockSpec(memory_space=pltpu.SMEM),
    ] + [_const_spec(w.shape) for w in vmem_weights]
    return pl.pallas_call(
        functools.partial(_mixer_kernel, alpha=alpha),
        grid=(B, S // TS),
        in_specs=in_specs,
        out_specs=pl.BlockSpec((1, TS, D), lambda b, s: (b, s, 0)),
        out_shape=jax.ShapeDtypeStruct((B, S, D), F32),
        scratch_shapes=[
            pltpu.VMEM((GLA_HEADS, DV, DK), F32),
            pltpu.VMEM((2 * SWA_KV_HEADS, SWA_WINDOW + TS, LANES), BF16),
            pltpu.VMEM((SWA_KV_HEADS, SWA_WINDOW + TS, LANES), BF16),
            pltpu.VMEM((TS, D), F32),
            pltpu.VMEM((TS, D), F32),
        ],
        compiler_params=pltpu.CompilerParams(
            dimension_semantics=("arbitrary", "arbitrary"), vmem_limit_bytes=VMEM_LIMIT_BYTES),
        name="token_mixer",
    )(x, positions.reshape(B, 1, S), mod, invf, sinks, *vmem_weights)


def _ffn_kernel(x_ref, mod_ref, wup_ref, cw_ref, cb_ref, wdn_ref, g_ref, b_ref, o_ref, u_ref, *, alpha):
    TS = x_ref.shape[1]
    F = wdn_ref.shape[0]
    PAD = SUBLANES
    s_idx = pl.program_id(1)

    @pl.when(s_idx == 0)
    def _():
        u_ref[0:PAD, :] = jnp.zeros((PAD, u_ref.shape[1]), F32)

    x = x_ref[0]
    mod = mod_ref[0]
    shift2, scale2, gate2 = mod[3:4], mod[4:5], mod[5:6]
    h2 = (_layer_norm(x) * (1.0 + scale2) + shift2).astype(BF16)
    u_ref[PAD:PAD + TS, :] = _dot(h2, wup_ref[...])
    cw = cw_ref[...]
    u = cb_ref[...]
    for t in range(CONV_W):
        off = PAD - (CONV_W - 1) + t
        u = u + cw[t:t + 1] * u_ref[off:off + TS, :]
    u_ref[0:PAD, :] = u_ref[TS:TS + PAD, :]
    u_gate, u_val = u[:, 0:F], u[:, F:2 * F]
    f = 0.5 * u_gate * (1.0 + lax.erf(u_gate * (2.0 ** -0.5))) * u_val
    t_out = _dot(f.astype(BF16), wdn_ref[...])
    o_ref[0] = _layer_norm(alpha * x + gate2 * t_out) * g_ref[...] + b_ref[...]


def _channel_mixer(x, mod, wup, cw, cb, wdn, g, b, *, alpha):
    B, S, D = x.shape
    TS = SEQ_TILE
    consts = (wup, cw, cb, wdn, g, b)
    return pl.pallas_call(
        functools.partial(_ffn_kernel, alpha=alpha),
        grid=(B, S // TS),
        in_specs=[
            pl.BlockSpec((1, TS, D), lambda b_, s: (b_, s, 0)),
            pl.BlockSpec((1, N_MOD, D), lambda b_, s: (b_, 0, 0)),
        ] + [_const_spec(w.shape) for w in consts],
        out_specs=pl.BlockSpec((1, TS, D), lambda b_, s: (b_, s, 0)),
        out_shape=jax.ShapeDtypeStruct((B, S, D), F32),
        scratch_shapes=[pltpu.VMEM((SUBLANES + TS, wup.shape[1]), F32)],
        compiler_params=pltpu.CompilerParams(
            dimension_semantics=("arbitrary", "arbitrary"), vmem_limit_bytes=VMEM_LIMIT_BYTES),
        name="channel_mixer",
    )(x, mod, *consts)


def kernel(x, c, positions, w_ada, b_ada, w_in, gla_w_lr, gla_b_lr, gla_norm_g, swa_sinks,
           w_o, ln1_g, ln1_b, w_up, conv_w, conv_b, w_down, ln2_g, ln2_b):
    B, S, D = x.shape
    depth = w_ada.shape[0]
    alpha = (2.0 * depth) ** 0.25
    assert S % SEQ_TILE == 0 and SEQ_TILE % SWA_WINDOW == 0 and SEQ_TILE % GLA_CHUNK == 0
    assert D % LANES == 0 and (N_MOD * D) % MOD_COL_TILE == 0

    dk_tot = gla_w_lr.shape[2]
    dv_tot = D
    n_q = D
    n_kv = SWA_KV_HEADS * SWA_HD
    assert n_kv == LANES
    splits = (dk_tot, dk_tot, dv_tot, dv_tot, GLA_RANK, n_q, n_kv, n_kv, 2 * D)
    offs = [0]
    for n in splits:
        offs.append(offs[-1] + n)
    assert offs[-1] == w_in.shape[2]

    inv_freq = ROPE_THETA ** (-(jnp.arange(0, ROPE_DIM, 2, dtype=F32) / ROPE_DIM))
    head_pat = jnp.concatenate([inv_freq, inv_freq, jnp.zeros((SWA_HD - ROPE_DIM,), F32)])
    invf = jnp.tile(head_pat, LANES // SWA_HD).reshape(1, LANES)

    for l in range(depth):
        mod = _modulation(c, w_ada[l], b_ada[l]).reshape(B, N_MOD, D)
        wl = w_in[l]
        col = lambda i: wl[:, offs[i]:offs[i + 1]].astype(BF16)
        wlra = jnp.pad(col(4), ((0, 0), (0, LANES - GLA_RANK)))
        wlr = jnp.pad(gla_w_lr[l].astype(BF16), ((0, LANES - GLA_RANK), (0, 0)))
        wkvb = wl[:, offs[6]:offs[8]].astype(BF16)
        weights = (invf, swa_sinks[l], col(0), col(1), col(2), col(3), wlra, wlr,
                   gla_b_lr[l].reshape(1, -1), gla_norm_g[l].reshape(1, -1),
                   col(5), wkvb, col(8), w_o[l].astype(BF16),
                   ln1_g[l].reshape(1, D), ln1_b[l].reshape(1, D))
        x = _token_mixer(x, positions, mod, weights, alpha=alpha)
        x = _channel_mixer(x, mod, w_up[l].astype(BF16), conv_w[l], conv_b[l].reshape(1, -1),
                           w_down[l].astype(BF16), ln2_g[l].reshape(1, D), ln2_b[l].reshape(1, D),
                           alpha=alpha)
    return x
```

```python
import functools

import jax
import jax.numpy as jnp
from jax import lax
from jax.experimental import pallas as pl
from jax.experimental.pallas import tpu as pltpu

LANES = 128
SUBLANES = 8
MXU_COLS = 256
VMEM_LIMIT_BYTES = 56 * 1024 * 1024

GLA_HEADS = 4
GLA_RANK = 16
GLA_TAU = 16.0
GLA_CHUNK = 64
SWA_HD = 64
SWA_KV_HEADS = 2
SWA_WINDOW = 128
ROPE_DIM = SWA_HD // 4
ROPE_THETA = 500000.0
CONV_W = 3
LN_EPS = 1e-5
RMS_EPS = 1e-6
N_MOD = 6

SEQ_TILE = 256
MOD_COL_TILE = 1536

BF16 = jnp.bfloat16
F32 = jnp.float32


def _dot(a, b):
    return jnp.dot(a, b, preferred_element_type=F32)


def _dot_nt(a, b):
    return lax.dot_general(a, b, (((1,), (1,)), ((), ())), preferred_element_type=F32)


def _dot_tn(a, b):
    return lax.dot_general(a, b, (((0,), (0,)), ((), ())), preferred_element_type=F32)


def _layer_norm(x):
    mu = jnp.mean(x, axis=-1, keepdims=True)
    xc = x - mu
    var = jnp.mean(xc * xc, axis=-1, keepdims=True)
    return xc * lax.rsqrt(var + LN_EPS)


def _split3_bf16(a):
    hi = a.astype(BF16)
    r1 = a - hi.astype(F32)
    mid = r1.astype(BF16)
    lo = (r1 - mid.astype(F32)).astype(BF16)
    return hi, mid, lo


def _mod_kernel(c_ref, w_ref, b_ref, o_ref):
    c = c_ref[...]
    c_act = c * jax.nn.sigmoid(c)
    o_ref[...] = _dot(c_act.astype(BF16), w_ref[...].astype(BF16)) + b_ref[...]


def _modulation(c, w_ada, b_ada):
    B, D = c.shape
    N = w_ada.shape[1]
    return pl.pallas_call(
        _mod_kernel,
        grid=(N // MOD_COL_TILE,),
        in_specs=[
            pl.BlockSpec((B, D), lambda j: (0, 0)),
            pl.BlockSpec((D, MOD_COL_TILE), lambda j: (0, j)),
            pl.BlockSpec((1, MOD_COL_TILE), lambda j: (0, j)),
        ],
        out_specs=pl.BlockSpec((B, MOD_COL_TILE), lambda j: (0, j)),
        out_shape=jax.ShapeDtypeStruct((B, N), F32),
        compiler_params=pltpu.CompilerParams(
            dimension_semantics=("arbitrary",), vmem_limit_bytes=VMEM_LIMIT_BYTES),
        name="adaln_mod",
    )(c, w_ada, b_ada.reshape(1, N))


def _gla_gates(h, wlra_ref, wlr_ref, blr_ref):
    lra = _dot(h, wlra_ref[...]).astype(BF16)
    z = _dot(lra, wlr_ref[...]) + blr_ref[...]
    return (jnp.minimum(z, 0.0) - jnp.log1p(jnp.exp(-jnp.abs(z)))) * (1.0 / GLA_TAU)


def _gla_factors(gcum, qa, ka):
    TS, HD = gcum.shape
    C = GLA_CHUNK
    nc = TS // C
    g3 = gcum.reshape(nc, C, HD)
    g_mid = g3[:, C // 2 - 1:C // 2, :]
    g_last = g3[:, C - 1:C, :]
    q_mid = qa.reshape(nc, C, HD) * jnp.exp(g3 - g_mid)
    k_mid = ka.reshape(nc, C, HD) * jnp.exp(g_mid - g3)
    q_in = q_mid * jnp.exp(g_mid)
    k_out = k_mid * jnp.exp(g_last - g_mid)
    prefix = [jnp.zeros((1, HD), F32)]
    for c in range(nc):
        prefix.append(prefix[-1] + g_last[c])
    q_mid_b = q_mid.reshape(TS, HD).astype(BF16)
    k_mid_b = k_mid.reshape(TS, HD).astype(BF16)
    q_in_b = q_in.reshape(TS, HD).astype(BF16)
    q_s0 = jnp.concatenate([q_in[c] * jnp.exp(prefix[c]) for c in range(nc)], axis=0).astype(BF16)
    k_end = jnp.concatenate([k_out[c] * jnp.exp(prefix[nc] - prefix[c + 1]) for c in range(nc)],
                            axis=0).astype(BF16)
    k_cross = [None]
    for c in range(1, nc):
        parts = [k_out[cp] * jnp.exp(prefix[c] - prefix[cp + 1]) for cp in range(c)]
        parts.append(jnp.zeros((TS - c * C, HD), F32))
        k_cross.append(jnp.concatenate(parts, axis=0).astype(BF16))
    decay_end = jnp.exp(prefix[nc])
    return q_mid_b, k_mid_b, q_in_b, q_s0, k_end, k_cross, decay_end


def _gla_head(hh, factors, va, ra, gnorm, state_ref, ya_ref):
    q_mid_b, k_mid_b, q_in_b, q_s0, k_end, k_cross, decay_end = factors
    TS, HD = q_mid_b.shape
    C = GLA_CHUNK
    nc = TS // C
    DK = HD // GLA_HEADS
    DV = va.shape[1] // GLA_HEADS
    kc = slice(hh * DK, (hh + 1) * DK)
    vc = slice(hh * DV, (hh + 1) * DV)
    ri = lax.broadcasted_iota(jnp.int32, (TS, TS), 0)
    ci = lax.broadcasted_iota(jnp.int32, (TS, TS), 1)
    same_chunk_causal = (ri // C == ci // C) & (ci <= ri)
    v_h = va[:, vc]
    s_diag = _dot_nt(q_mid_b[:, kc], k_mid_b[:, kc])
    cross = [jnp.zeros((C, TS), F32)]
    for c in range(1, nc):
        cross.append(_dot_nt(q_in_b[c * C:(c + 1) * C, kc], k_cross[c][:, kc]))
    a = jnp.where(same_chunk_causal, s_diag, jnp.concatenate(cross, axis=0)).astype(BF16)
    st = state_ref[hh]
    o = _dot(a, v_h) + _dot(q_s0[:, kc], st.astype(BF16))
    d_col = jnp.broadcast_to(decay_end[:, kc], (DK, DK)).T
    state_ref[hh] = st * jnp.concatenate([d_col] * (DV // DK), axis=1) + _dot_tn(k_end[:, kc], v_h)
    rms = lax.rsqrt(jnp.mean(o * o, axis=-1, keepdims=True) + RMS_EPS)
    r_h = ra[:, vc]
    ya_ref[:, vc] = (o * rms) * gnorm * (r_h * jax.nn.sigmoid(r_h))


def _swa_project(h, pos_ref, invf_ref, wqbt_ref, wkvt_ref, kb_ref, vb_ref):
    TS = h.shape[0]
    W = SWA_WINDOW
    HD = SWA_HD
    n_heads = wqbt_ref.shape[0] // HD
    half = ROPE_DIM // 2
    ang = invf_ref[...] * pos_ref[0].astype(F32)
    cos_t, sin_t = jnp.cos(ang), jnp.sin(ang)

    def rope(t, heads):
        t3 = t.reshape(heads, HD, TS)
        t1, t2 = t3[:, 0:half, :], t3[:, half:ROPE_DIM, :]
        return jnp.concatenate([t1 * cos_t - t2 * sin_t, t2 * cos_t + t1 * sin_t, t3[:, ROPE_DIM:, :]], axis=1)

    q3 = rope(_dot_nt(wqbt_ref[...], h), n_heads).astype(BF16)
    kvt = _dot_nt(wkvt_ref[...], h)
    n_kv = SWA_KV_HEADS * HD
    k_rot = rope(kvt[0:n_kv], SWA_KV_HEADS).reshape(n_kv, TS)
    kb_ref[W:W + TS, :] = k_rot.T.astype(BF16)
    vb_ref[:, W:W + TS] = kvt[n_kv:2 * n_kv].astype(BF16)
    return q3


def _swa_scores(j, g, q3, kb_ref):
    W = SWA_WINDOW
    per_kv = q3.shape[0] // SWA_KV_HEADS
    qcols = slice(j * W, (j + 1) * W)
    k_band = kb_ref[j * W:j * W + 2 * W, :]
    q_g = jnp.concatenate([q3[g * per_kv + e, :, qcols] for e in range(per_kv)], axis=1)
    zeros_q = jnp.zeros_like(q_g)
    q_pad = jnp.concatenate([zeros_q] * g + [q_g] + [zeros_q] * (SWA_KV_HEADS - 1 - g), axis=0)
    return _dot(k_band, q_pad)


def _swa_softmax_pv(j, g, sc_all, s_idx, sinks_ref, vb_ref, ybt_ref, between_heads):
    W = SWA_WINDOW
    HD = SWA_HD
    per_kv = sc_all.shape[1] // W
    kj = lax.broadcasted_iota(jnp.int32, (2 * W, W), 0)
    qi = lax.broadcasted_iota(jnp.int32, (2 * W, W), 1)
    mask = (kj > qi) & (kj <= qi + W)
    if j == 0:
        mask = mask & ((kj >= W) | (s_idx > 0))
    p_parts, inv_parts = [], []
    for e in range(per_kv):
        sink = sinks_ref[g * per_kv + e]
        sc = jnp.where(mask, sc_all[:, e * W:(e + 1) * W], -jnp.inf)
        m = jnp.maximum(jnp.max(sc, axis=0, keepdims=True), sink)
        pexp = jnp.exp(sc - m)
        denom = jnp.sum(pexp, axis=0, keepdims=True) + jnp.exp(sink - m)
        p_parts.append(pexp.astype(BF16))
        inv_parts.append(1.0 / denom)
        between_heads(e)
    v_band = vb_ref[g * HD:(g + 1) * HD, j * W:j * W + 2 * W]
    o_t = _dot(v_band, jnp.concatenate(p_parts, axis=1)) * jnp.concatenate(inv_parts, axis=1)
    for e in range(per_kv):
        hd = g * per_kv + e
        ybt_ref[hd * HD:(hd + 1) * HD, j * W:(j + 1) * W] = o_t[:, e * W:(e + 1) * W]


def _mixer_kernel(x_ref, pos_ref, mod_ref, invf_ref, sinks_ref, tri3_ref,
                  wqa_ref, wka_ref, wva_ref, wra_ref, wlra_ref, wlr_ref, blr_ref, gnorm_ref,
                  wqbt_ref, wkvt_ref, wg_ref, wo_ref, ln1g_ref, ln1b_ref,
                  o_ref,
                  state_ref, kb_ref, vb_ref, ya_ref, ybt_ref, qk_ref, va_ref, ra_ref, gates_ref,
                  *, alpha):
    TS, D = x_ref.shape[1], x_ref.shape[2]
    W = SWA_WINDOW
    HD = wqa_ref.shape[1]
    DK = HD // GLA_HEADS
    s_idx = pl.program_id(1)

    @pl.when(s_idx == 0)
    def _():
        state_ref[...] = jnp.zeros_like(state_ref)
        kb_ref[0:W, :] = jnp.zeros((W, kb_ref.shape[1]), BF16)
        vb_ref[:, 0:W] = jnp.zeros((vb_ref.shape[0], W), BF16)

    x = x_ref[0]
    mod = mod_ref[0]
    shift1, scale1, gate1 = mod[0:1], mod[1:2], mod[2:3]
    h = (_layer_norm(x) * (1.0 + scale1) + shift1).astype(BF16)

    def chunks(dst_ref, col0, w_ref, post):
        def make(c0):
            def run():
                dst_ref[:, col0 + c0:col0 + c0 + MXU_COLS] = post(_dot(h, w_ref[:, c0:c0 + MXU_COLS]))
            return run
        return [make(c0) for c0 in range(0, w_ref.shape[1], MXU_COLS)]

    pending = (chunks(qk_ref, 0, wqa_ref, lambda v: v * (DK ** -0.5)) + chunks(qk_ref, HD, wka_ref, lambda v: v)
               + chunks(va_ref, 0, wva_ref, lambda v: v.astype(BF16)) + chunks(ra_ref, 0, wra_ref, lambda v: v)
               + chunks(gates_ref, 0, wg_ref, lambda v: v))
    n_qk = 2 * HD // MXU_COLS
    n_gla = n_qk + 2 * wva_ref.shape[1] // MXU_COLS
    emitted = [0]

    def emit_upto(n):
        while emitted[0] < min(n, len(pending)):
            pending[emitted[0]]()
            emitted[0] += 1

    log_a = _gla_gates(h, wlra_ref, wlr_ref, blr_ref)
    q3 = _swa_project(h, pos_ref, invf_ref, wqbt_ref, wkvt_ref, kb_ref, vb_ref)
    emit_upto(n_qk)
    gcum = _dot(tri3_ref[...], jnp.concatenate(_split3_bf16(log_a), axis=0))
    gnorm = gnorm_ref[...]
    blocks = [(j, g) for j in range(TS // W) for g in range(SWA_KV_HEADS)]
    nb = len(blocks)
    per_kv = q3.shape[0] // SWA_KV_HEADS
    head_blocks = blocks[nb // 2:]
    heads_per_block = -(-GLA_HEADS // len(head_blocks))
    gla_next = [0]
    sc = _swa_scores(*blocks[0], q3, kb_ref)
    factors = None
    for bi, (j, g) in enumerate(blocks):
        sc_next = _swa_scores(*blocks[bi + 1], q3, kb_ref) if bi + 1 < nb else None

        def between_heads(e, bi=bi):
            if e % 2 == 0:
                emit_upto(emitted[0] + 1)
            elif bi >= nb // 2 and (e + 1) % (per_kv // heads_per_block) == 0 and gla_next[0] < GLA_HEADS:
                emit_upto(n_gla)
                _gla_head(gla_next[0], factors, va_ref, ra_ref, gnorm, state_ref, ya_ref)
                gla_next[0] += 1

        _swa_softmax_pv(j, g, sc, s_idx, sinks_ref, vb_ref, ybt_ref, between_heads)
        sc = sc_next
        if bi == 0:
            factors = _gla_factors(gcum, qk_ref[:, 0:HD], qk_ref[:, HD:2 * HD])
    emit_upto(len(pending))
    while gla_next[0] < GLA_HEADS:
        _gla_head(gla_next[0], factors, va_ref, ra_ref, gnorm, state_ref, ya_ref)
        gla_next[0] += 1
    kb_ref[0:W, :] = kb_ref[TS:TS + W, :]
    vb_ref[:, 0:W] = vb_ref[:, TS:TS + W]

    y = (jax.nn.sigmoid(gates_ref[:, 0:D]) * ya_ref[...]
         + jax.nn.sigmoid(gates_ref[:, D:2 * D]) * ybt_ref[...].T)
    t = _dot(y.astype(BF16), wo_ref[...])
    o_ref[0] = _layer_norm(alpha * x + gate1 * t) * ln1g_ref[...] + ln1b_ref[...]


def _const_spec(shape):
    nd = len(shape)
    return pl.BlockSpec(shape, lambda b, s: (0,) * nd)


def _token_mixer(x, positions, mod, weights, *, alpha):
    B, S, D = x.shape
    TS = SEQ_TILE
    (invf, sinks, tri3, wqa, wka, wva, wra, wlra, wlr, blr, gnorm, wqbt, wkvt, wg, wo, ln1g, ln1b) = weights
    vmem_consts = (tri3, wqa, wka, wva, wra, wlra, wlr, blr, gnorm, wqbt, wkvt, wg, wo, ln1g, ln1b)
    DK = wqa.shape[1] // GLA_HEADS
    DV = wva.shape[1] // GLA_HEADS
    n_kv = SWA_KV_HEADS * SWA_HD
    in_specs = [
        pl.BlockSpec((1, TS, D), lambda b, s: (b, s, 0)),
        pl.BlockSpec((1, 1, TS), lambda b, s: (b, 0, s)),
        pl.BlockSpec((1, N_MOD, D), lambda b, s: (b, 0, 0)),
        _const_spec(invf.shape),
        pl.BlockSpec(memory_space=pltpu.SMEM),
    ] + [_const_spec(w.shape) for w in vmem_consts]
    return pl.pallas_call(
        functools.partial(_mixer_kernel, alpha=alpha),
        grid=(B, S // TS),
        in_specs=in_specs,
        out_specs=pl.BlockSpec((1, TS, D), lambda b, s: (b, s, 0)),
        out_shape=jax.ShapeDtypeStruct((B, S, D), F32),
        scratch_shapes=[
            pltpu.VMEM((GLA_HEADS, DK, DV), F32),
            pltpu.VMEM((SWA_WINDOW + TS, n_kv), BF16),
            pltpu.VMEM((n_kv, SWA_WINDOW + TS), BF16),
            pltpu.VMEM((TS, D), F32),
            pltpu.VMEM((wqbt.shape[0], TS), F32),
            pltpu.VMEM((TS, wqa.shape[1] + wka.shape[1]), F32),
            pltpu.VMEM((TS, wva.shape[1]), BF16),
            pltpu.VMEM((TS, wra.shape[1]), F32),
            pltpu.VMEM((TS, wg.shape[1]), F32),
        ],
        compiler_params=pltpu.CompilerParams(
            dimension_semantics=("arbitrary", "arbitrary"), vmem_limit_bytes=VMEM_LIMIT_BYTES),
        name="token_mixer",
    )(x, positions.reshape(B, 1, S), mod, invf, sinks, *vmem_consts)


def _ffn_kernel(x_ref, mod_ref, wup_ref, cw_ref, cb_ref, wdn_ref, g_ref, b_ref, o_ref, u_ref, *, alpha):
    TS = x_ref.shape[1]
    F = wdn_ref.shape[0]
    PAD = SUBLANES
    s_idx = pl.program_id(1)

    @pl.when(s_idx == 0)
    def _():
        u_ref[0:PAD, :] = jnp.zeros((PAD, u_ref.shape[1]), F32)

    x = x_ref[0]
    mod = mod_ref[0]
    shift2, scale2, gate2 = mod[3:4], mod[4:5], mod[5:6]
    h2 = (_layer_norm(x) * (1.0 + scale2) + shift2).astype(BF16)
    u_ref[PAD:PAD + TS, :] = _dot(h2, wup_ref[...])
    cw = cw_ref[...]
    u = cb_ref[...]
    for t in range(CONV_W):
        off = PAD - (CONV_W - 1) + t
        u = u + cw[t:t + 1] * u_ref[off:off + TS, :]
    u_ref[0:PAD, :] = u_ref[TS:TS + PAD, :]
    u_gate, u_val = u[:, 0:F], u[:, F:2 * F]
    f = 0.5 * u_gate * (1.0 + lax.erf(u_gate * (2.0 ** -0.5))) * u_val
    t_out = _dot(f.astype(BF16), wdn_ref[...])
    o_ref[0] = _layer_norm(alpha * x + gate2 * t_out) * g_ref[...] + b_ref[...]


def _channel_mixer(x, mod, wup, cw, cb, wdn, g, b, *, alpha):
    B, S, D = x.shape
    TS = SEQ_TILE
    consts = (wup, cw, cb, wdn, g, b)
    return pl.pallas_call(
        functools.partial(_ffn_kernel, alpha=alpha),
        grid=(B, S // TS),
        in_specs=[
            pl.BlockSpec((1, TS, D), lambda b_, s: (b_, s, 0)),
            pl.BlockSpec((1, N_MOD, D), lambda b_, s: (b_, 0, 0)),
        ] + [_const_spec(w.shape) for w in consts],
        out_specs=pl.BlockSpec((1, TS, D), lambda b_, s: (b_, s, 0)),
        out_shape=jax.ShapeDtypeStruct((B, S, D), F32),
        scratch_shapes=[pltpu.VMEM((SUBLANES + TS, wup.shape[1]), F32)],
        compiler_params=pltpu.CompilerParams(
            dimension_semantics=("arbitrary", "arbitrary"), vmem_limit_bytes=VMEM_LIMIT_BYTES),
        name="channel_mixer",
    )(x, mod, *consts)


def kernel(x, c, positions, w_ada, b_ada, w_in, gla_w_lr, gla_b_lr, gla_norm_g, swa_sinks,
           w_o, ln1_g, ln1_b, w_up, conv_w, conv_b, w_down, ln2_g, ln2_b):
    B, S, D = x.shape
    depth = w_ada.shape[0]
    alpha = (2.0 * depth) ** 0.25
    assert S % SEQ_TILE == 0 and SEQ_TILE % SWA_WINDOW == 0 and SEQ_TILE % GLA_CHUNK == 0
    assert D % LANES == 0 and (N_MOD * D) % MOD_COL_TILE == 0

    dk_tot = gla_w_lr.shape[2]
    dv_tot = D
    n_q = D
    n_kv = SWA_KV_HEADS * SWA_HD
    assert n_kv == LANES
    splits = (dk_tot, dk_tot, dv_tot, dv_tot, GLA_RANK, n_q, n_kv, n_kv, 2 * D)
    offs = [0]
    for n in splits:
        offs.append(offs[-1] + n)
    assert offs[-1] == w_in.shape[2]

    inv_freq = ROPE_THETA ** (-(jnp.arange(0, ROPE_DIM, 2, dtype=F32) / ROPE_DIM))
    invf = inv_freq.reshape(ROPE_DIM // 2, 1)
    tok = jnp.arange(SEQ_TILE)
    tri = (tok[:, None] // GLA_CHUNK == tok[None, :] // GLA_CHUNK) & (tok[None, :] <= tok[:, None])
    tri3 = jnp.tile(tri.astype(BF16), (1, 3))

    for l in range(depth):
        mod = _modulation(c, w_ada[l], b_ada[l]).reshape(B, N_MOD, D)
        wl = w_in[l]
        col = lambda i: wl[:, offs[i]:offs[i + 1]].astype(BF16)
        wlra = jnp.pad(col(4), ((0, 0), (0, LANES - GLA_RANK)))
        wlr = jnp.pad(gla_w_lr[l].astype(BF16), ((0, LANES - GLA_RANK), (0, 0)))
        wqbt = (wl[:, offs[5]:offs[6]] * (SWA_HD ** -0.5)).T.astype(BF16)
        wkvt = wl[:, offs[6]:offs[8]].T.astype(BF16)
        weights = (invf, swa_sinks[l], tri3, col(0), col(1), col(2), col(3), wlra, wlr,
                   gla_b_lr[l].reshape(1, -1), gla_norm_g[l].reshape(1, -1),
                   wqbt, wkvt, col(8), w_o[l].astype(BF16),
                   ln1_g[l].reshape(1, D), ln1_b[l].reshape(1, D))
        x = _token_mixer(x, positions, mod, weights, alpha=alpha)
        x = _channel_mixer(x, mod, w_up[l].astype(BF16), conv_w[l], conv_b[l].reshape(1, -1),
                           w_down[l].astype(BF16), ln2_g[l].reshape(1, D), ln2_b[l].reshape(1, D),
                           alpha=alpha)
    return x
```

```python
import functools

import jax
import jax.numpy as jnp
from jax import lax
from jax.experimental import pallas as pl
from jax.experimental.pallas import tpu as pltpu

LANES = 128
SUBLANES = 8
MXU_COLS = 256
VMEM_LIMIT_BYTES = 56 * 1024 * 1024

GLA_HEADS = 4
GLA_RANK = 16
GLA_TAU = 16.0
GLA_CHUNK = 64
SWA_HD = 64
SWA_KV_HEADS = 2
SWA_WINDOW = 128
ROPE_DIM = SWA_HD // 4
ROPE_THETA = 500000.0
CONV_W = 3
LN_EPS = 1e-5
RMS_EPS = 1e-6
N_MOD = 6

SUB_TILE = 256
SEQ_TILE = 2 * SUB_TILE
MOD_COL_TILE = 1536

BF16 = jnp.bfloat16
F32 = jnp.float32


def _dot(a, b):
    return jnp.dot(a, b, preferred_element_type=F32)


def _dot_nt(a, b):
    return lax.dot_general(a, b, (((1,), (1,)), ((), ())), preferred_element_type=F32)


def _dot_tn(a, b):
    return lax.dot_general(a, b, (((0,), (0,)), ((), ())), preferred_element_type=F32)


def _layer_norm(x):
    mu = jnp.mean(x, axis=-1, keepdims=True)
    xc = x - mu
    var = jnp.mean(xc * xc, axis=-1, keepdims=True)
    return xc * lax.rsqrt(var + LN_EPS)


def _split3_bf16(a):
    hi = a.astype(BF16)
    r1 = a - hi.astype(F32)
    mid = r1.astype(BF16)
    lo = (r1 - mid.astype(F32)).astype(BF16)
    return hi, mid, lo


def _mod_kernel(c_ref, w_ref, b_ref, o_ref):
    c = c_ref[...]
    c_act = c * jax.nn.sigmoid(c)
    o_ref[...] = _dot(c_act.astype(BF16), w_ref[...].astype(BF16)) + b_ref[...]


def _modulation(c, w_ada, b_ada):
    B, D = c.shape
    N = w_ada.shape[1]
    return pl.pallas_call(
        _mod_kernel,
        grid=(N // MOD_COL_TILE,),
        in_specs=[
            pl.BlockSpec((B, D), lambda j: (0, 0)),
            pl.BlockSpec((D, MOD_COL_TILE), lambda j: (0, j)),
            pl.BlockSpec((1, MOD_COL_TILE), lambda j: (0, j)),
        ],
        out_specs=pl.BlockSpec((B, MOD_COL_TILE), lambda j: (0, j)),
        out_shape=jax.ShapeDtypeStruct((B, N), F32),
        compiler_params=pltpu.CompilerParams(
            dimension_semantics=("arbitrary",), vmem_limit_bytes=VMEM_LIMIT_BYTES),
        name="adaln_mod",
    )(c, w_ada, b_ada.reshape(1, N))


def _gla_gates(h, wlra_ref, wlr_ref, blr_ref):
    lra = _dot(h, wlra_ref[...]).astype(BF16)
    z = _dot(lra, wlr_ref[...]) + blr_ref[...]
    return (jnp.minimum(z, 0.0) - jnp.log1p(jnp.exp(-jnp.abs(z)))) * (1.0 / GLA_TAU)


def _gla_factors(gcum, qa, ka):
    TS, HD = gcum.shape
    C = GLA_CHUNK
    nc = TS // C
    g3 = gcum.reshape(nc, C, HD)
    g_mid = g3[:, C // 2 - 1:C // 2, :]
    g_last = g3[:, C - 1:C, :]
    q_mid = qa.reshape(nc, C, HD) * jnp.exp(g3 - g_mid)
    k_mid = ka.reshape(nc, C, HD) * jnp.exp(g_mid - g3)
    q_in = q_mid * jnp.exp(g_mid)
    k_out = k_mid * jnp.exp(g_last - g_mid)
    prefix = [jnp.zeros((1, HD), F32)]
    for c in range(nc):
        prefix.append(prefix[-1] + g_last[c])
    q_mid_b = q_mid.reshape(TS, HD).astype(BF16)
    k_mid_b = k_mid.reshape(TS, HD).astype(BF16)
    q_in_b = q_in.reshape(TS, HD).astype(BF16)
    q_s0 = jnp.concatenate([q_in[c] * jnp.exp(prefix[c]) for c in range(nc)], axis=0).astype(BF16)
    k_end = jnp.concatenate([k_out[c] * jnp.exp(prefix[nc] - prefix[c + 1]) for c in range(nc)],
                            axis=0).astype(BF16)
    k_cross = [None]
    for c in range(1, nc):
        parts = [k_out[cp] * jnp.exp(prefix[c] - prefix[cp + 1]) for cp in range(c)]
        parts.append(jnp.zeros((TS - c * C, HD), F32))
        k_cross.append(jnp.concatenate(parts, axis=0).astype(BF16))
    decay_end = jnp.exp(prefix[nc])
    return q_mid_b, k_mid_b, q_in_b, q_s0, k_end, k_cross, decay_end


def _gla_head(hh, factors, va, ra, gnorm, state_ref, ya_ref):
    q_mid_b, k_mid_b, q_in_b, q_s0, k_end, k_cross, decay_end = factors
    TS, HD = q_mid_b.shape
    C = GLA_CHUNK
    nc = TS // C
    DK = HD // GLA_HEADS
    DV = va.shape[1] // GLA_HEADS
    kc = slice(hh * DK, (hh + 1) * DK)
    vc = slice(hh * DV, (hh + 1) * DV)
    ri = lax.broadcasted_iota(jnp.int32, (TS, TS), 0)
    ci = lax.broadcasted_iota(jnp.int32, (TS, TS), 1)
    same_chunk_causal = (ri // C == ci // C) & (ci <= ri)
    v_h = va[:, vc]
    s_diag = _dot_nt(q_mid_b[:, kc], k_mid_b[:, kc])
    cross = [jnp.zeros((C, TS), F32)]
    for c in range(1, nc):
        cross.append(_dot_nt(q_in_b[c * C:(c + 1) * C, kc], k_cross[c][:, kc]))
    a = jnp.where(same_chunk_causal, s_diag, jnp.concatenate(cross, axis=0)).astype(BF16)
    st = state_ref[hh]
    o = _dot(a, v_h) + _dot(q_s0[:, kc], st.astype(BF16))
    d_col = jnp.broadcast_to(decay_end[:, kc], (DK, DK)).T
    state_ref[hh] = st * jnp.concatenate([d_col] * (DV // DK), axis=1) + _dot_tn(k_end[:, kc], v_h)
    rms = lax.rsqrt(jnp.mean(o * o, axis=-1, keepdims=True) + RMS_EPS)
    r_h = ra[:, vc]
    ya_ref[:, vc] = (o * rms) * gnorm * (r_h * jax.nn.sigmoid(r_h))


def _swa_project(h, pos, r0, invf_ref, wqbt_ref, wkvt_ref, kb_ref, vb_ref):
    TS = h.shape[0]
    W = SWA_WINDOW
    HD = SWA_HD
    n_heads = wqbt_ref.shape[0] // HD
    half = ROPE_DIM // 2
    ang = invf_ref[...] * pos.astype(F32)
    cos_t, sin_t = jnp.cos(ang), jnp.sin(ang)

    def rope(t, heads):
        t3 = t.reshape(heads, HD, TS)
        t1, t2 = t3[:, 0:half, :], t3[:, half:ROPE_DIM, :]
        return jnp.concatenate([t1 * cos_t - t2 * sin_t, t2 * cos_t + t1 * sin_t, t3[:, ROPE_DIM:, :]], axis=1)

    q3 = rope(_dot_nt(wqbt_ref[...], h), n_heads).astype(BF16)
    kvt = _dot_nt(wkvt_ref[...], h)
    n_kv = SWA_KV_HEADS * HD
    k_rot = rope(kvt[0:n_kv], SWA_KV_HEADS).reshape(n_kv, TS)
    kb_ref[W + r0:W + r0 + TS, :] = k_rot.T.astype(BF16)
    vb_ref[:, W + r0:W + r0 + TS] = kvt[n_kv:2 * n_kv].astype(BF16)
    return q3


def _swa_scores(j, jl, g, q3, kb_ref):
    W = SWA_WINDOW
    per_kv = q3.shape[0] // SWA_KV_HEADS
    qcols = slice(jl * W, (jl + 1) * W)
    k_band = kb_ref[j * W:j * W + 2 * W, :]
    q_g = jnp.concatenate([q3[g * per_kv + e, :, qcols] for e in range(per_kv)], axis=1)
    zeros_q = jnp.zeros_like(q_g)
    q_pad = jnp.concatenate([zeros_q] * g + [q_g] + [zeros_q] * (SWA_KV_HEADS - 1 - g), axis=0)
    return _dot(k_band, q_pad)


def _swa_softmax_pv(j, jl, g, sc_all, s_idx, sinks_ref, vb_ref, ybt_ref, between_heads):
    W = SWA_WINDOW
    HD = SWA_HD
    per_kv = sc_all.shape[1] // W
    kj = lax.broadcasted_iota(jnp.int32, (2 * W, W), 0)
    qi = lax.broadcasted_iota(jnp.int32, (2 * W, W), 1)
    mask = (kj > qi) & (kj <= qi + W)
    if j == 0:
        mask = mask & ((kj >= W) | (s_idx > 0))
    p_parts, inv_parts = [], []
    for e in range(per_kv):
        sink = sinks_ref[g * per_kv + e]
        sc = jnp.where(mask, sc_all[:, e * W:(e + 1) * W], -jnp.inf)
        m = jnp.maximum(jnp.max(sc, axis=0, keepdims=True), sink)
        pexp = jnp.exp(sc - m)
        denom = jnp.sum(pexp, axis=0, keepdims=True) + jnp.exp(sink - m)
        p_parts.append(pexp.astype(BF16))
        inv_parts.append(1.0 / denom)
        between_heads(e)
    v_band = vb_ref[g * HD:(g + 1) * HD, j * W:j * W + 2 * W]
    o_t = _dot(v_band, jnp.concatenate(p_parts, axis=1)) * jnp.concatenate(inv_parts, axis=1)
    for e in range(per_kv):
        hd = g * per_kv + e
        ybt_ref[hd * HD:(hd + 1) * HD, jl * W:(jl + 1) * W] = o_t[:, e * W:(e + 1) * W]


def _mixer_kernel(x_ref, pos_ref, mod_ref, invf_ref, sinks_ref, tri3_ref,
                  wqa_ref, wka_ref, wva_ref, wra_ref, wlra_ref, wlr_ref, blr_ref, gnorm_ref,
                  wqbt_ref, wkvt_ref, wg_ref, wo_ref, ln1g_ref, ln1b_ref,
                  o_ref,
                  state_ref, kb_ref, vb_ref, ya_ref, ybt_ref, qk_ref, va_ref, ra_ref, gates_ref,
                  *, alpha):
    TS, D = x_ref.shape[1], x_ref.shape[2]
    W = SWA_WINDOW
    s_idx = pl.program_id(1)

    @pl.when(s_idx == 0)
    def _():
        state_ref[...] = jnp.zeros_like(state_ref)
        kb_ref[0:W, :] = jnp.zeros((W, kb_ref.shape[1]), BF16)
        vb_ref[:, 0:W] = jnp.zeros((vb_ref.shape[0], W), BF16)

    refs = (pos_ref, mod_ref, invf_ref, sinks_ref, tri3_ref,
            wqa_ref, wka_ref, wva_ref, wra_ref, wlra_ref, wlr_ref, blr_ref, gnorm_ref,
            wqbt_ref, wkvt_ref, wg_ref, wo_ref, ln1g_ref, ln1b_ref, state_ref, kb_ref, vb_ref)
    for sub in range(TS // SUB_TILE):
        scratch = tuple(r.at[sub] for r in (ya_ref, ybt_ref, qk_ref, va_ref, ra_ref, gates_ref))
        _mixer_subtile(sub, s_idx, x_ref, o_ref, refs, scratch, alpha)
    kb_ref[0:W, :] = kb_ref[TS:TS + W, :]
    vb_ref[:, 0:W] = vb_ref[:, TS:TS + W]


def _mixer_subtile(sub, s_idx, x_ref, o_ref, refs, scratch, alpha):
    (pos_ref, mod_ref, invf_ref, sinks_ref, tri3_ref,
     wqa_ref, wka_ref, wva_ref, wra_ref, wlra_ref, wlr_ref, blr_ref, gnorm_ref,
     wqbt_ref, wkvt_ref, wg_ref, wo_ref, ln1g_ref, ln1b_ref, state_ref, kb_ref, vb_ref) = refs
    ya_ref, ybt_ref, qk_ref, va_ref, ra_ref, gates_ref = scratch
    TS = SUB_TILE
    D = x_ref.shape[2]
    W = SWA_WINDOW
    HD = wqa_ref.shape[1]
    DK = HD // GLA_HEADS
    r0 = sub * TS
    rows = slice(r0, r0 + TS)

    x = x_ref[0, rows, :]
    mod = mod_ref[0]
    shift1, scale1, gate1 = mod[0:1], mod[1:2], mod[2:3]
    h = (_layer_norm(x) * (1.0 + scale1) + shift1).astype(BF16)

    def chunks(dst_ref, col0, w_ref, post):
        def make(c0):
            def run():
                dst_ref[:, col0 + c0:col0 + c0 + MXU_COLS] = post(_dot(h, w_ref[:, c0:c0 + MXU_COLS]))
            return run
        return [make(c0) for c0 in range(0, w_ref.shape[1], MXU_COLS)]

    pending = (chunks(qk_ref, 0, wqa_ref, lambda v: v * (DK ** -0.5)) + chunks(qk_ref, HD, wka_ref, lambda v: v)
               + chunks(va_ref, 0, wva_ref, lambda v: v.astype(BF16)) + chunks(ra_ref, 0, wra_ref, lambda v: v)
               + chunks(gates_ref, 0, wg_ref, lambda v: v))
    n_qk = 2 * HD // MXU_COLS
    n_gla = n_qk + 2 * wva_ref.shape[1] // MXU_COLS
    emitted = [0]

    def emit_upto(n):
        while emitted[0] < min(n, len(pending)):
            pending[emitted[0]]()
            emitted[0] += 1

    log_a = _gla_gates(h, wlra_ref, wlr_ref, blr_ref)
    q3 = _swa_project(h, pos_ref[0, :, rows], r0, invf_ref, wqbt_ref, wkvt_ref, kb_ref, vb_ref)
    emit_upto(n_qk)
    gcum = _dot(tri3_ref[...], jnp.concatenate(_split3_bf16(log_a), axis=0))
    gnorm = gnorm_ref[...]
    j0 = r0 // W
    blocks = [(j0 + jl, jl, g) for jl in range(TS // W) for g in range(SWA_KV_HEADS)]
    nb = len(blocks)
    per_kv = q3.shape[0] // SWA_KV_HEADS
    head_blocks = blocks[nb // 2:]
    heads_per_block = -(-GLA_HEADS // len(head_blocks))
    gla_next = [0]
    sc = _swa_scores(*blocks[0], q3, kb_ref)
    factors = None
    for bi, (j, jl, g) in enumerate(blocks):
        sc_next = _swa_scores(*blocks[bi + 1], q3, kb_ref) if bi + 1 < nb else None

        def between_heads(e, bi=bi):
            if e % 2 == 0:
                emit_upto(emitted[0] + 1)
            elif bi >= nb // 2 and (e + 1) % (per_kv // heads_per_block) == 0 and gla_next[0] < GLA_HEADS:
                emit_upto(n_gla)
                _gla_head(gla_next[0], factors, va_ref, ra_ref, gnorm, state_ref, ya_ref)
                gla_next[0] += 1

        _swa_softmax_pv(j, jl, g, sc, s_idx, sinks_ref, vb_ref, ybt_ref, between_heads)
        sc = sc_next
        if bi == 0:
            factors = _gla_factors(gcum, qk_ref[:, 0:HD], qk_ref[:, HD:2 * HD])
    emit_upto(len(pending))
    while gla_next[0] < GLA_HEADS:
        _gla_head(gla_next[0], factors, va_ref, ra_ref, gnorm, state_ref, ya_ref)
        gla_next[0] += 1

    y = (jax.nn.sigmoid(gates_ref[:, 0:D]) * ya_ref[...]
         + jax.nn.sigmoid(gates_ref[:, D:2 * D]) * ybt_ref[...].T)
    t = _dot(y.astype(BF16), wo_ref[...])
    o_ref[0, rows, :] = _layer_norm(alpha * x + gate1 * t) * ln1g_ref[...] + ln1b_ref[...]


def _const_spec(shape):
    nd = len(shape)
    return pl.BlockSpec(shape, lambda b, s: (0,) * nd)


def _token_mixer(x, positions, mod, weights, *, alpha):
    B, S, D = x.shape
    TS, ST = SEQ_TILE, SUB_TILE
    NS = TS // ST
    (invf, sinks, tri3, wqa, wka, wva, wra, wlra, wlr, blr, gnorm, wqbt, wkvt, wg, wo, ln1g, ln1b) = weights
    vmem_consts = (tri3, wqa, wka, wva, wra, wlra, wlr, blr, gnorm, wqbt, wkvt, wg, wo, ln1g, ln1b)
    DK = wqa.shape[1] // GLA_HEADS
    DV = wva.shape[1] // GLA_HEADS
    n_kv = SWA_KV_HEADS * SWA_HD
    in_specs = [
        pl.BlockSpec((1, TS, D), lambda b, s: (b, s, 0)),
        pl.BlockSpec((1, 1, TS), lambda b, s: (b, 0, s)),
        pl.BlockSpec((1, N_MOD, D), lambda b, s: (b, 0, 0)),
        _const_spec(invf.shape),
        pl.BlockSpec(memory_space=pltpu.SMEM),
    ] + [_const_spec(w.shape) for w in vmem_consts]
    return pl.pallas_call(
        functools.partial(_mixer_kernel, alpha=alpha),
        grid=(B, S // TS),
        in_specs=in_specs,
        out_specs=pl.BlockSpec((1, TS, D), lambda b, s: (b, s, 0)),
        out_shape=jax.ShapeDtypeStruct((B, S, D), F32),
        scratch_shapes=[
            pltpu.VMEM((GLA_HEADS, DK, DV), F32),
            pltpu.VMEM((SWA_WINDOW + TS, n_kv), BF16),
            pltpu.VMEM((n_kv, SWA_WINDOW + TS), BF16),
            pltpu.VMEM((NS, ST, D), F32),
            pltpu.VMEM((NS, wqbt.shape[0], ST), F32),
            pltpu.VMEM((NS, ST, wqa.shape[1] + wka.shape[1]), F32),
            pltpu.VMEM((NS, ST, wva.shape[1]), BF16),
            pltpu.VMEM((NS, ST, wra.shape[1]), F32),
            pltpu.VMEM((NS, ST, wg.shape[1]), F32),
        ],
        compiler_params=pltpu.CompilerParams(
            dimension_semantics=("arbitrary", "arbitrary"), vmem_limit_bytes=VMEM_LIMIT_BYTES),
        name="token_mixer",
    )(x, positions.reshape(B, 1, S), mod, invf, sinks, *vmem_consts)


def _ffn_kernel(x_ref, mod_ref, wup_ref, cw_ref, cb_ref, wdn_ref, g_ref, b_ref, o_ref, u_ref, *, alpha):
    TS = x_ref.shape[1]
    F = wdn_ref.shape[0]
    PAD = SUBLANES
    s_idx = pl.program_id(1)

    @pl.when(s_idx == 0)
    def _():
        u_ref[0:PAD, :] = jnp.zeros((PAD, u_ref.shape[1]), F32)

    mod = mod_ref[0]
    shift2, scale2, gate2 = mod[3:4], mod[4:5], mod[5:6]
    x = x_ref[0]
    h2 = (_layer_norm(x) * (1.0 + scale2) + shift2).astype(BF16)
    u_ref[PAD:PAD + TS, :] = _dot(h2, wup_ref[...])
    cw = cw_ref[...]
    u = cb_ref[...]
    for t in range(CONV_W):
        off = PAD - (CONV_W - 1) + t
        u = u + cw[t:t + 1] * u_ref[off:off + TS, :]
    u_ref[0:PAD, :] = u_ref[TS:TS + PAD, :]
    u_gate, u_val = u[:, 0:F], u[:, F:2 * F]
    f = 0.5 * u_gate * (1.0 + lax.erf(u_gate * (2.0 ** -0.5))) * u_val
    t_out = _dot(f.astype(BF16), wdn_ref[...])
    o_ref[0] = _layer_norm(alpha * x + gate2 * t_out) * g_ref[...] + b_ref[...]


def _channel_mixer(x, mod, wup, cw, cb, wdn, g, b, *, alpha):
    B, S, D = x.shape
    TS = SEQ_TILE
    consts = (wup, cw, cb, wdn, g, b)
    return pl.pallas_call(
        functools.partial(_ffn_kernel, alpha=alpha),
        grid=(B, S // TS),
        in_specs=[
            pl.BlockSpec((1, TS, D), lambda b_, s: (b_, s, 0)),
            pl.BlockSpec((1, N_MOD, D), lambda b_, s: (b_, 0, 0)),
        ] + [_const_spec(w.shape) for w in consts],
        out_specs=pl.BlockSpec((1, TS, D), lambda b_, s: (b_, s, 0)),
        out_shape=jax.ShapeDtypeStruct((B, S, D), F32),
        scratch_shapes=[pltpu.VMEM((SUBLANES + TS, wup.shape[1]), F32)],
        compiler_params=pltpu.CompilerParams(
            dimension_semantics=("arbitrary", "arbitrary"), vmem_limit_bytes=VMEM_LIMIT_BYTES),
        name="channel_mixer",
    )(x, mod, *consts)


def kernel(x, c, positions, w_ada, b_ada, w_in, gla_w_lr, gla_b_lr, gla_norm_g, swa_sinks,
           w_o, ln1_g, ln1_b, w_up, conv_w, conv_b, w_down, ln2_g, ln2_b):
    B, S, D = x.shape
    depth = w_ada.shape[0]
    alpha = (2.0 * depth) ** 0.25
    assert S % SEQ_TILE == 0 and SUB_TILE % SWA_WINDOW == 0 and SUB_TILE % GLA_CHUNK == 0
    assert D % LANES == 0 and (N_MOD * D) % MOD_COL_TILE == 0

    dk_tot = gla_w_lr.shape[2]
    dv_tot = D
    n_q = D
    n_kv = SWA_KV_HEADS * SWA_HD
    assert n_kv == LANES
    splits = (dk_tot, dk_tot, dv_tot, dv_tot, GLA_RANK, n_q, n_kv, n_kv, 2 * D)
    offs = [0]
    for n in splits:
        offs.append(offs[-1] + n)
    assert offs[-1] == w_in.shape[2]

    inv_freq = ROPE_THETA ** (-(jnp.arange(0, ROPE_DIM, 2, dtype=F32) / ROPE_DIM))
    invf = inv_freq.reshape(ROPE_DIM // 2, 1)
    tok = jnp.arange(SUB_TILE)
    tri = (tok[:, None] // GLA_CHUNK == tok[None, :] // GLA_CHUNK) & (tok[None, :] <= tok[:, None])
    tri3 = jnp.tile(tri.astype(BF16), (1, 3))

    for l in range(depth):
        mod = _modulation(c, w_ada[l], b_ada[l]).reshape(B, N_MOD, D)
        wl = w_in[l]
        col = lambda i: wl[:, offs[i]:offs[i + 1]].astype(BF16)
        wlra = jnp.pad(col(4), ((0, 0), (0, LANES - GLA_RANK)))
        wlr = jnp.pad(gla_w_lr[l].astype(BF16), ((0, LANES - GLA_RANK), (0, 0)))
        wqbt = (wl[:, offs[5]:offs[6]] * (SWA_HD ** -0.5)).T.astype(BF16)
        wkvt = wl[:, offs[6]:offs[8]].T.astype(BF16)
        weights = (invf, swa_sinks[l], tri3, col(0), col(1), col(2), col(3), wlra, wlr,
                   gla_b_lr[l].reshape(1, -1), gla_norm_g[l].reshape(1, -1),
                   wqbt, wkvt, col(8), w_o[l].astype(BF16),
                   ln1_g[l].reshape(1, D), ln1_b[l].reshape(1, D))
        x = _token_mixer(x, positions, mod, weights, alpha=alpha)
        x = _channel_mixer(x, mod, w_up[l].astype(BF16), conv_w[l], conv_b[l].reshape(1, -1),
                           w_down[l].astype(BF16), ln2_g[l].reshape(1, D), ln2_b[l].reshape(1, D),
                           alpha=alpha)
    return x
```

```python
import functools

import jax
import jax.numpy as jnp
from jax import lax
from jax.experimental import pallas as pl
from jax.experimental.pallas import tpu as pltpu

LANES = 128
SUBLANES = 8
MXU_COLS = 256
VMEM_LIMIT_BYTES = 56 * 1024 * 1024

GLA_HEADS = 4
GLA_RANK = 16
GLA_TAU = 16.0
GLA_CHUNK = 64
SWA_HD = 64
SWA_KV_HEADS = 2
SWA_WINDOW = 128
ROPE_DIM = SWA_HD // 4
ROPE_THETA = 500000.0
CONV_W = 3
LN_EPS = 1e-5
RMS_EPS = 1e-6
N_MOD = 6
LOG2E = 1.4426950408889634

SUB_TILE = 256
SEQ_TILE = 2 * SUB_TILE
MOD_COL_TILE = 1536

BF16 = jnp.bfloat16
F32 = jnp.float32


def _dot(a, b):
    return jnp.dot(a, b, preferred_element_type=F32)


def _dot_nt(a, b):
    return lax.dot_general(a, b, (((1,), (1,)), ((), ())), preferred_element_type=F32)


def _dot_tn(a, b):
    return lax.dot_general(a, b, (((0,), (0,)), ((), ())), preferred_element_type=F32)


def _layer_norm(x):
    mu = jnp.mean(x, axis=-1, keepdims=True)
    xc = x - mu
    var = jnp.mean(xc * xc, axis=-1, keepdims=True)
    return xc * lax.rsqrt(var + LN_EPS)


def _split3_bf16(a):
    hi = a.astype(BF16)
    r1 = a - hi.astype(F32)
    mid = r1.astype(BF16)
    lo = (r1 - mid.astype(F32)).astype(BF16)
    return hi, mid, lo


def _mod_kernel(c_ref, w_ref, b_ref, o_ref):
    c = c_ref[...]
    c_act = c * jax.nn.sigmoid(c)
    o_ref[...] = _dot(c_act.astype(BF16), w_ref[...].astype(BF16)) + b_ref[...]


def _modulation(c, w_ada, b_ada):
    B, D = c.shape
    N = w_ada.shape[1]
    return pl.pallas_call(
        _mod_kernel,
        grid=(N // MOD_COL_TILE,),
        in_specs=[
            pl.BlockSpec((B, D), lambda j: (0, 0)),
            pl.BlockSpec((D, MOD_COL_TILE), lambda j: (0, j)),
            pl.BlockSpec((1, MOD_COL_TILE), lambda j: (0, j)),
        ],
        out_specs=pl.BlockSpec((B, MOD_COL_TILE), lambda j: (0, j)),
        out_shape=jax.ShapeDtypeStruct((B, N), F32),
        compiler_params=pltpu.CompilerParams(
            dimension_semantics=("arbitrary",), vmem_limit_bytes=VMEM_LIMIT_BYTES),
        name="adaln_mod",
    )(c, w_ada, b_ada.reshape(1, N))


def _gla_gates(h, wlra_ref, wlr_ref, blr_ref):
    lra = _dot(h, wlra_ref[...]).astype(BF16)
    z = _dot(lra, wlr_ref[...]) + blr_ref[...]
    return (jnp.minimum(z, 0.0) - jnp.log1p(jnp.exp(-jnp.abs(z)))) * (1.0 / GLA_TAU)


def _gla_factors(gcum, qa, ka):
    TS, HD = gcum.shape
    C = GLA_CHUNK
    nc = TS // C
    g3 = gcum.reshape(nc, C, HD)
    g_mid = g3[:, C // 2 - 1:C // 2, :]
    g_last = g3[:, C - 1:C, :]
    q_mid = qa.reshape(nc, C, HD) * jnp.exp(g3 - g_mid)
    k_mid = ka.reshape(nc, C, HD) * jnp.exp(g_mid - g3)
    q_in = q_mid * jnp.exp(g_mid)
    k_out = k_mid * jnp.exp(g_last - g_mid)
    prefix = [jnp.zeros((1, HD), F32)]
    for c in range(nc):
        prefix.append(prefix[-1] + g_last[c])
    q_mid_b = q_mid.reshape(TS, HD).astype(BF16)
    k_mid_b = k_mid.reshape(TS, HD).astype(BF16)
    q_in_b = q_in.reshape(TS, HD).astype(BF16)
    q_s0 = jnp.concatenate([q_in[c] * jnp.exp(prefix[c]) for c in range(nc)], axis=0).astype(BF16)
    k_end = jnp.concatenate([k_out[c] * jnp.exp(prefix[nc] - prefix[c + 1]) for c in range(nc)],
                            axis=0).astype(BF16)
    k_cross = [None]
    for c in range(1, nc):
        parts = [k_out[cp] * jnp.exp(prefix[c] - prefix[cp + 1]) for cp in range(c)]
        parts.append(jnp.zeros((TS - c * C, HD), F32))
        k_cross.append(jnp.concatenate(parts, axis=0).astype(BF16))
    decay_end = jnp.exp(prefix[nc])
    return q_mid_b, k_mid_b, q_in_b, q_s0, k_end, k_cross, decay_end


def _gla_head(hh, factors, va, ra, gnorm, state_ref, ya_ref):
    q_mid_b, k_mid_b, q_in_b, q_s0, k_end, k_cross, decay_end = factors
    TS, HD = q_mid_b.shape
    C = GLA_CHUNK
    nc = TS // C
    DK = HD // GLA_HEADS
    DV = va.shape[1] // GLA_HEADS
    kc = slice(hh * DK, (hh + 1) * DK)
    vc = slice(hh * DV, (hh + 1) * DV)
    ri = lax.broadcasted_iota(jnp.int32, (TS, TS), 0)
    ci = lax.broadcasted_iota(jnp.int32, (TS, TS), 1)
    same_chunk_causal = (ri // C == ci // C) & (ci <= ri)
    v_h = va[:, vc]
    s_diag = _dot_nt(q_mid_b[:, kc], k_mid_b[:, kc])
    cross = [jnp.zeros((C, TS), F32)]
    for c in range(1, nc):
        cross.append(_dot_nt(q_in_b[c * C:(c + 1) * C, kc], k_cross[c][:, kc]))
    a = jnp.where(same_chunk_causal, s_diag, jnp.concatenate(cross, axis=0)).astype(BF16)
    st = state_ref[hh]
    o = _dot(a, v_h) + _dot(q_s0[:, kc], st.astype(BF16))
    d_col = jnp.broadcast_to(decay_end[:, kc], (DK, DK)).T
    state_ref[hh] = st * jnp.concatenate([d_col] * (DV // DK), axis=1) + _dot_tn(k_end[:, kc], v_h)
    rms = lax.rsqrt(jnp.mean(o * o, axis=-1, keepdims=True) + RMS_EPS)
    r_h = ra[:, vc]
    ya_ref[:, vc] = (o * rms) * gnorm * (r_h * jax.nn.sigmoid(r_h))


def _swa_project(h, pos, r0, invf_ref, wqbt_ref, wkvt_ref, kb_ref, vb_ref):
    TS = h.shape[0]
    W = SWA_WINDOW
    HD = SWA_HD
    n_heads = wqbt_ref.shape[0] // HD
    half = ROPE_DIM // 2
    ang = invf_ref[...] * pos.astype(F32)
    cos_t, sin_t = jnp.cos(ang), jnp.sin(ang)

    def rope(t, heads):
        t3 = t.reshape(heads, HD, TS)
        t1, t2 = t3[:, 0:half, :], t3[:, half:ROPE_DIM, :]
        return jnp.concatenate([t1 * cos_t - t2 * sin_t, t2 * cos_t + t1 * sin_t, t3[:, ROPE_DIM:, :]], axis=1)

    q3 = rope(_dot_nt(wqbt_ref[...], h), n_heads).astype(BF16)
    kvt = _dot_nt(wkvt_ref[...], h)
    n_kv = SWA_KV_HEADS * HD
    k_rot = rope(kvt[0:n_kv], SWA_KV_HEADS).reshape(n_kv, TS)
    kb_ref[W + r0:W + r0 + TS, :] = k_rot.T.astype(BF16)
    vb_ref[:, W + r0:W + r0 + TS] = kvt[n_kv:2 * n_kv].astype(BF16)
    return q3


def _swa_scores(j, jl, g, q3, kb_ref):
    W = SWA_WINDOW
    per_kv = q3.shape[0] // SWA_KV_HEADS
    qcols = slice(jl * W, (jl + 1) * W)
    k_band = kb_ref[j * W:j * W + 2 * W, :]
    q_g = jnp.concatenate([q3[g * per_kv + e, :, qcols] for e in range(per_kv)], axis=1)
    zeros_q = jnp.zeros_like(q_g)
    q_pad = jnp.concatenate([zeros_q] * g + [q_g] + [zeros_q] * (SWA_KV_HEADS - 1 - g), axis=0)
    return _dot(k_band, q_pad)


def _swa_softmax_pv(j, jl, g, sc_all, s_idx, sinks_ref, vb_ref, ybt_ref, between_heads):
    W = SWA_WINDOW
    HD = SWA_HD
    per_kv = sc_all.shape[1] // W
    kj = lax.broadcasted_iota(jnp.int32, (W, W), 0)
    qi = lax.broadcasted_iota(jnp.int32, (W, W), 1)
    from_prev = kj > qi
    p_parts, inv_parts = [], []
    for e in range(per_kv):
        sink = sinks_ref[g * per_kv + e] * LOG2E
        s_prev = sc_all[0:W, e * W:(e + 1) * W]
        if j == 0:
            s_prev = jnp.where(s_idx > 0, s_prev, -jnp.inf)
        sc = jnp.where(from_prev, s_prev, sc_all[W:2 * W, e * W:(e + 1) * W])
        m = jnp.maximum(jnp.max(sc, axis=0, keepdims=True), sink)
        pexp = jnp.exp2(sc - m)
        denom = jnp.sum(pexp, axis=0, keepdims=True) + jnp.exp2(sink - m)
        p_parts.append(jnp.concatenate([jnp.where(from_prev, pexp, 0.0), jnp.where(from_prev, 0.0, pexp)],
                                       axis=0).astype(BF16))
        inv_parts.append(1.0 / denom)
        between_heads(e)
    v_band = vb_ref[g * HD:(g + 1) * HD, j * W:j * W + 2 * W]
    o_t = _dot(v_band, jnp.concatenate(p_parts, axis=1)) * jnp.concatenate(inv_parts, axis=1)
    for e in range(per_kv):
        hd = g * per_kv + e
        ybt_ref[hd * HD:(hd + 1) * HD, jl * W:(jl + 1) * W] = o_t[:, e * W:(e + 1) * W]


def _mixer_kernel(x_ref, pos_ref, mod_ref, invf_ref, sinks_ref, tri3_ref,
                  wtok_ref, wfeat_ref, wlr_ref, blr_ref, gnorm_ref, wo_ref, ln1g_ref, ln1b_ref,
                  o_ref,
                  state_ref, kb_ref, vb_ref, ya_ref, ybt_ref, qk_ref, va_ref, ra_ref, gates_ref,
                  *, alpha, tok_widths, feat_heights):
    TS, D = x_ref.shape[1], x_ref.shape[2]
    W = SWA_WINDOW
    s_idx = pl.program_id(1)

    tok_offs = [sum(tok_widths[:i]) for i in range(len(tok_widths) + 1)]
    wqa_ref, wka_ref, wva_ref, wra_ref, wlra_ref, wg_ref = (
        wtok_ref.at[:, tok_offs[i]:tok_offs[i + 1]] for i in range(len(tok_widths)))
    feat_offs = [sum(feat_heights[:i]) for i in range(len(feat_heights) + 1)]
    wqbt_ref, wkvt_ref = (wfeat_ref.at[feat_offs[i]:feat_offs[i + 1], :] for i in range(len(feat_heights)))

    @pl.when(s_idx == 0)
    def _():
        state_ref[...] = jnp.zeros_like(state_ref)
        kb_ref[0:W, :] = jnp.zeros((W, kb_ref.shape[1]), BF16)
        vb_ref[:, 0:W] = jnp.zeros((vb_ref.shape[0], W), BF16)

    refs = (pos_ref, mod_ref, invf_ref, sinks_ref, tri3_ref,
            wqa_ref, wka_ref, wva_ref, wra_ref, wlra_ref, wlr_ref, blr_ref, gnorm_ref,
            wqbt_ref, wkvt_ref, wg_ref, wo_ref, ln1g_ref, ln1b_ref, state_ref, kb_ref, vb_ref)
    for sub in range(TS // SUB_TILE):
        scratch = tuple(r.at[sub] for r in (ya_ref, ybt_ref, qk_ref, va_ref, ra_ref, gates_ref))
        _mixer_subtile(sub, s_idx, x_ref, o_ref, refs, scratch, alpha)
    kb_ref[0:W, :] = kb_ref[TS:TS + W, :]
    vb_ref[:, 0:W] = vb_ref[:, TS:TS + W]


def _mixer_subtile(sub, s_idx, x_ref, o_ref, refs, scratch, alpha):
    (pos_ref, mod_ref, invf_ref, sinks_ref, tri3_ref,
     wqa_ref, wka_ref, wva_ref, wra_ref, wlra_ref, wlr_ref, blr_ref, gnorm_ref,
     wqbt_ref, wkvt_ref, wg_ref, wo_ref, ln1g_ref, ln1b_ref, state_ref, kb_ref, vb_ref) = refs
    ya_ref, ybt_ref, qk_ref, va_ref, ra_ref, gates_ref = scratch
    TS = SUB_TILE
    D = x_ref.shape[2]
    W = SWA_WINDOW
    HD = wqa_ref.shape[1]
    DK = HD // GLA_HEADS
    r0 = sub * TS
    rows = slice(r0, r0 + TS)

    x = x_ref[0, rows, :]
    mod = mod_ref[0]
    shift1, scale1, gate1 = mod[0:1], mod[1:2], mod[2:3]
    h = (_layer_norm(x) * (1.0 + scale1) + shift1).astype(BF16)

    def chunks(dst_ref, col0, w_ref, post):
        def make(c0):
            def run():
                dst_ref[:, col0 + c0:col0 + c0 + MXU_COLS] = post(_dot(h, w_ref[:, c0:c0 + MXU_COLS]))
            return run
        return [make(c0) for c0 in range(0, w_ref.shape[1], MXU_COLS)]

    pending = (chunks(qk_ref, 0, wqa_ref, lambda v: v * (DK ** -0.5)) + chunks(qk_ref, HD, wka_ref, lambda v: v)
               + chunks(va_ref, 0, wva_ref, lambda v: v.astype(BF16)) + chunks(ra_ref, 0, wra_ref, lambda v: v)
               + chunks(gates_ref, 0, wg_ref, lambda v: v))
    n_qk = 2 * HD // MXU_COLS
    n_gla = n_qk + 2 * wva_ref.shape[1] // MXU_COLS
    emitted = [0]

    def emit_upto(n):
        while emitted[0] < min(n, len(pending)):
            pending[emitted[0]]()
            emitted[0] += 1

    log_a = _gla_gates(h, wlra_ref, wlr_ref, blr_ref)
    q3 = _swa_project(h, pos_ref[0, :, rows], r0, invf_ref, wqbt_ref, wkvt_ref, kb_ref, vb_ref)
    emit_upto(n_qk)
    gcum = _dot(tri3_ref[...], jnp.concatenate(_split3_bf16(log_a), axis=0))
    gnorm = gnorm_ref[...]
    j0 = r0 // W
    blocks = [(j0 + jl, jl, g) for jl in range(TS // W) for g in range(SWA_KV_HEADS)]
    nb = len(blocks)
    per_kv = q3.shape[0] // SWA_KV_HEADS
    head_blocks = blocks[nb // 2:]
    heads_per_block = -(-GLA_HEADS // len(head_blocks))
    gla_next = [0]
    sc = _swa_scores(*blocks[0], q3, kb_ref)
    factors = None
    for bi, (j, jl, g) in enumerate(blocks):
        sc_next = _swa_scores(*blocks[bi + 1], q3, kb_ref) if bi + 1 < nb else None

        def between_heads(e, bi=bi):
            if e % 2 == 0:
                emit_upto(emitted[0] + 1)
            elif bi >= nb // 2 and (e + 1) % (per_kv // heads_per_block) == 0 and gla_next[0] < GLA_HEADS:
                emit_upto(n_gla)
                _gla_head(gla_next[0], factors, va_ref, ra_ref, gnorm, state_ref, ya_ref)
                gla_next[0] += 1

        _swa_softmax_pv(j, jl, g, sc, s_idx, sinks_ref, vb_ref, ybt_ref, between_heads)
        sc = sc_next
        if bi == 0:
            factors = _gla_factors(gcum, qk_ref[:, 0:HD], qk_ref[:, HD:2 * HD])
    emit_upto(len(pending))
    while gla_next[0] < GLA_HEADS:
        _gla_head(gla_next[0], factors, va_ref, ra_ref, gnorm, state_ref, ya_ref)
        gla_next[0] += 1

    y = (jax.nn.sigmoid(gates_ref[:, 0:D]) * ya_ref[...]
         + jax.nn.sigmoid(gates_ref[:, D:2 * D]) * ybt_ref[...].T)
    t = _dot(y.astype(BF16), wo_ref[...])
    o_ref[0, rows, :] = _layer_norm(alpha * x + gate1 * t) * ln1g_ref[...] + ln1b_ref[...]


def _const_spec(shape):
    nd = len(shape)
    return pl.BlockSpec(shape, lambda b, s: (0,) * nd)


def _token_mixer(x, positions, mod, weights, *, alpha):
    B, S, D = x.shape
    TS, ST = SEQ_TILE, SUB_TILE
    NS = TS // ST
    (invf, sinks, tri3, wtok, wfeat, wlr, blr, gnorm, wo, ln1g, ln1b, tok_widths, feat_heights) = weights
    vmem_consts = (tri3, wtok, wfeat, wlr, blr, gnorm, wo, ln1g, ln1b)
    n_qa, n_ka, n_va, n_ra, _, n_g = tok_widths
    DK = n_qa // GLA_HEADS
    DV = n_va // GLA_HEADS
    n_kv = SWA_KV_HEADS * SWA_HD
    in_specs = [
        pl.BlockSpec((1, TS, D), lambda b, s: (b, s, 0)),
        pl.BlockSpec((1, 1, TS), lambda b, s: (b, 0, s)),
        pl.BlockSpec((1, N_MOD, D), lambda b, s: (b, 0, 0)),
        _const_spec(invf.shape),
        pl.BlockSpec(memory_space=pltpu.SMEM),
    ] + [_const_spec(w.shape) for w in vmem_consts]
    return pl.pallas_call(
        functools.partial(_mixer_kernel, alpha=alpha, tok_widths=tok_widths, feat_heights=feat_heights),
        grid=(B, S // TS),
        in_specs=in_specs,
        out_specs=pl.BlockSpec((1, TS, D), lambda b, s: (b, s, 0)),
        out_shape=jax.ShapeDtypeStruct((B, S, D), F32),
        scratch_shapes=[
            pltpu.VMEM((GLA_HEADS, DK, DV), F32),
            pltpu.VMEM((SWA_WINDOW + TS, n_kv), BF16),
            pltpu.VMEM((n_kv, SWA_WINDOW + TS), BF16),
            pltpu.VMEM((NS, ST, D), F32),
            pltpu.VMEM((NS, feat_heights[0], ST), F32),
            pltpu.VMEM((NS, ST, n_qa + n_ka), F32),
            pltpu.VMEM((NS, ST, n_va), BF16),
            pltpu.VMEM((NS, ST, n_ra), F32),
            pltpu.VMEM((NS, ST, n_g), F32),
        ],
        compiler_params=pltpu.CompilerParams(
            dimension_semantics=("arbitrary", "arbitrary"), vmem_limit_bytes=VMEM_LIMIT_BYTES),
        name="token_mixer",
    )(x, positions.reshape(B, 1, S), mod, invf, sinks, *vmem_consts)


def _ffn_kernel(x_ref, mod_ref, wup_ref, cw_ref, cb_ref, wdn_ref, g_ref, b_ref, o_ref, u_ref, *, alpha):
    TS = x_ref.shape[1]
    F = wdn_ref.shape[0]
    PAD = SUBLANES
    s_idx = pl.program_id(1)

    @pl.when(s_idx == 0)
    def _():
        u_ref[0:PAD, :] = jnp.zeros((PAD, u_ref.shape[1]), F32)

    mod = mod_ref[0]
    shift2, scale2, gate2 = mod[3:4], mod[4:5], mod[5:6]
    x = x_ref[0]
    h2 = (_layer_norm(x) * (1.0 + scale2) + shift2).astype(BF16)
    u_ref[PAD:PAD + TS, :] = _dot(h2, wup_ref[...])
    cw = cw_ref[...]
    u = cb_ref[...]
    for t in range(CONV_W):
        off = PAD - (CONV_W - 1) + t
        u = u + cw[t:t + 1] * u_ref[off:off + TS, :]
    u_ref[0:PAD, :] = u_ref[TS:TS + PAD, :]
    u_gate, u_val = u[:, 0:F], u[:, F:2 * F]
    f = 0.5 * u_gate * (1.0 + lax.erf(u_gate * (2.0 ** -0.5))) * u_val
    t_out = _dot(f.astype(BF16), wdn_ref[...])
    o_ref[0] = _layer_norm(alpha * x + gate2 * t_out) * g_ref[...] + b_ref[...]


def _channel_mixer(x, mod, wup, cw, cb, wdn, g, b, *, alpha):
    B, S, D = x.shape
    TS = SEQ_TILE
    consts = (wup, cw, cb, wdn, g, b)
    return pl.pallas_call(
        functools.partial(_ffn_kernel, alpha=alpha),
        grid=(B, S // TS),
        in_specs=[
            pl.BlockSpec((1, TS, D), lambda b_, s: (b_, s, 0)),
            pl.BlockSpec((1, N_MOD, D), lambda b_, s: (b_, 0, 0)),
        ] + [_const_spec(w.shape) for w in consts],
        out_specs=pl.BlockSpec((1, TS, D), lambda b_, s: (b_, s, 0)),
        out_shape=jax.ShapeDtypeStruct((B, S, D), F32),
        scratch_shapes=[pltpu.VMEM((SUBLANES + TS, wup.shape[1]), F32)],
        compiler_params=pltpu.CompilerParams(
            dimension_semantics=("arbitrary", "arbitrary"), vmem_limit_bytes=VMEM_LIMIT_BYTES),
        name="channel_mixer",
    )(x, mod, *consts)


def kernel(x, c, positions, w_ada, b_ada, w_in, gla_w_lr, gla_b_lr, gla_norm_g, swa_sinks,
           w_o, ln1_g, ln1_b, w_up, conv_w, conv_b, w_down, ln2_g, ln2_b):
    B, S, D = x.shape
    depth = w_ada.shape[0]
    alpha = (2.0 * depth) ** 0.25
    assert S % SEQ_TILE == 0 and SUB_TILE % SWA_WINDOW == 0 and SUB_TILE % GLA_CHUNK == 0
    assert D % LANES == 0 and (N_MOD * D) % MOD_COL_TILE == 0

    dk_tot = gla_w_lr.shape[2]
    dv_tot = D
    n_q = D
    n_kv = SWA_KV_HEADS * SWA_HD
    assert n_kv == LANES
    splits = (dk_tot, dk_tot, dv_tot, dv_tot, GLA_RANK, n_q, n_kv, n_kv, 2 * D)
    offs = [0]
    for n in splits:
        offs.append(offs[-1] + n)
    assert offs[-1] == w_in.shape[2]

    inv_freq = ROPE_THETA ** (-(jnp.arange(0, ROPE_DIM, 2, dtype=F32) / ROPE_DIM))
    invf = inv_freq.reshape(ROPE_DIM // 2, 1)
    tok = jnp.arange(SUB_TILE)
    tri = (tok[:, None] // GLA_CHUNK == tok[None, :] // GLA_CHUNK) & (tok[None, :] <= tok[:, None])
    tri3 = jnp.tile(tri.astype(BF16), (1, 3))

    for l in range(depth):
        mod = _modulation(c, w_ada[l], b_ada[l]).reshape(B, N_MOD, D)
        wl = w_in[l]
        col = lambda i: wl[:, offs[i]:offs[i + 1]]
        wlra = jnp.pad(col(4), ((0, 0), (0, LANES - GLA_RANK)))
        wtok = jnp.concatenate([col(0), col(1), col(2), col(3), wlra, col(8)], axis=1).astype(BF16)
        tok_widths = (dk_tot, dk_tot, dv_tot, dv_tot, LANES, 2 * D)
        wfeat = jnp.concatenate([col(5) * (SWA_HD ** -0.5 * LOG2E), col(6), col(7)], axis=1).T.astype(BF16)
        feat_heights = (n_q, 2 * n_kv)
        wlr = jnp.pad(gla_w_lr[l].astype(BF16), ((0, LANES - GLA_RANK), (0, 0)))
        weights = (invf, swa_sinks[l], tri3, wtok, wfeat, wlr,
                   gla_b_lr[l].reshape(1, -1), gla_norm_g[l].reshape(1, -1), w_o[l].astype(BF16),
                   ln1_g[l].reshape(1, D), ln1_b[l].reshape(1, D), tok_widths, feat_heights)
        x = _token_mixer(x, positions, mod, weights, alpha=alpha)
        x = _channel_mixer(x, mod, w_up[l].astype(BF16), conv_w[l], conv_b[l].reshape(1, -1),
                           w_down[l].astype(BF16), ln2_g[l].reshape(1, D), ln2_b[l].reshape(1, D),
                           alpha=alpha)
    return x
```

```python
import functools

import jax
import jax.numpy as jnp
from jax import lax
from jax.experimental import pallas as pl
from jax.experimental.pallas import tpu as pltpu

LANES = 128
SUBLANES = 8
MXU_COLS = 256
VMEM_LIMIT_BYTES = 56 * 1024 * 1024

GLA_HEADS = 4
GLA_RANK = 16
GLA_TAU = 16.0
GLA_CHUNK = 64
SWA_HD = 64
SWA_KV_HEADS = 2
SWA_WINDOW = 128
ROPE_DIM = SWA_HD // 4
ROPE_THETA = 500000.0
CONV_W = 3
LN_EPS = 1e-5
RMS_EPS = 1e-6
N_MOD = 6
LOG2E = 1.4426950408889634

SUB_TILE = 256
SEQ_TILE = 2 * SUB_TILE
MOD_COL_TILE = 1536

BF16 = jnp.bfloat16
F32 = jnp.float32


def _dot(a, b):
    return jnp.dot(a, b, preferred_element_type=F32)


def _dot_nt(a, b):
    return lax.dot_general(a, b, (((1,), (1,)), ((), ())), preferred_element_type=F32)


def _dot_tn(a, b):
    return lax.dot_general(a, b, (((0,), (0,)), ((), ())), preferred_element_type=F32)


def _layer_norm(x):
    mu = jnp.mean(x, axis=-1, keepdims=True)
    xc = x - mu
    var = jnp.mean(xc * xc, axis=-1, keepdims=True)
    return xc * lax.rsqrt(var + LN_EPS)


def _split3_bf16(a):
    hi = a.astype(BF16)
    r1 = a - hi.astype(F32)
    mid = r1.astype(BF16)
    lo = (r1 - mid.astype(F32)).astype(BF16)
    return hi, mid, lo


def _mod_kernel(c_ref, w_ref, b_ref, o_ref):
    c = c_ref[...]
    c_act = c * jax.nn.sigmoid(c)
    o_ref[...] = _dot(c_act.astype(BF16), w_ref[...].astype(BF16)) + b_ref[...]


def _modulation(c, w_ada, b_ada):
    B, D = c.shape
    N = w_ada.shape[1]
    return pl.pallas_call(
        _mod_kernel,
        grid=(N // MOD_COL_TILE,),
        in_specs=[
            pl.BlockSpec((B, D), lambda j: (0, 0)),
            pl.BlockSpec((D, MOD_COL_TILE), lambda j: (0, j)),
            pl.BlockSpec((1, MOD_COL_TILE), lambda j: (0, j)),
        ],
        out_specs=pl.BlockSpec((B, MOD_COL_TILE), lambda j: (0, j)),
        out_shape=jax.ShapeDtypeStruct((B, N), F32),
        compiler_params=pltpu.CompilerParams(
            dimension_semantics=("arbitrary",), vmem_limit_bytes=VMEM_LIMIT_BYTES),
        name="adaln_mod",
    )(c, w_ada, b_ada.reshape(1, N))


def _gla_gates(h, wlra_ref, wlr_ref, blr_ref):
    lra = _dot(h, wlra_ref[...]).astype(BF16)
    z = _dot(lra, wlr_ref[...]) + blr_ref[...]
    return (jnp.minimum(z, 0.0) - jnp.log1p(jnp.exp(-jnp.abs(z)))) * (1.0 / GLA_TAU)


def _gla_factors(gcum, qa, ka):
    TS, HD = gcum.shape
    C = GLA_CHUNK
    nc = TS // C
    g3 = gcum.reshape(nc, C, HD)
    g_mid = g3[:, C // 2 - 1:C // 2, :]
    g_last = g3[:, C - 1:C, :]
    q_mid = qa.reshape(nc, C, HD) * jnp.exp(g3 - g_mid)
    k_mid = ka.reshape(nc, C, HD) * jnp.exp(g_mid - g3)
    q_in = q_mid * jnp.exp(g_mid)
    k_out = k_mid * jnp.exp(g_last - g_mid)
    prefix = [jnp.zeros((1, HD), F32)]
    for c in range(nc):
        prefix.append(prefix[-1] + g_last[c])
    q_mid_b = q_mid.reshape(TS, HD).astype(BF16)
    k_mid_b = k_mid.reshape(TS, HD).astype(BF16)
    q_in_b = q_in.reshape(TS, HD).astype(BF16)
    q_s0 = jnp.concatenate([q_in[c] * jnp.exp(prefix[c]) for c in range(nc)], axis=0).astype(BF16)
    k_end = jnp.concatenate([k_out[c] * jnp.exp(prefix[nc] - prefix[c + 1]) for c in range(nc)],
                            axis=0).astype(BF16)
    k_cross = [None]
    for c in range(1, nc):
        parts = [k_out[cp] * jnp.exp(prefix[c] - prefix[cp + 1]) for cp in range(c)]
        parts.append(jnp.zeros((TS - c * C, HD), F32))
        k_cross.append(jnp.concatenate(parts, axis=0).astype(BF16))
    decay_end = jnp.exp(prefix[nc])
    return q_mid_b, k_mid_b, q_in_b, q_s0, k_end, k_cross, decay_end


def _gla_head(hh, factors, va, ra, gnorm, state_ref, ya_ref):
    q_mid_b, k_mid_b, q_in_b, q_s0, k_end, k_cross, decay_end = factors
    TS, HD = q_mid_b.shape
    C = GLA_CHUNK
    nc = TS // C
    DK = HD // GLA_HEADS
    DV = va.shape[1] // GLA_HEADS
    kc = slice(hh * DK, (hh + 1) * DK)
    vc = slice(hh * DV, (hh + 1) * DV)
    ri = lax.broadcasted_iota(jnp.int32, (TS, TS), 0)
    ci = lax.broadcasted_iota(jnp.int32, (TS, TS), 1)
    same_chunk_causal = (ri // C == ci // C) & (ci <= ri)
    v_h = va[:, vc]
    s_diag = _dot_nt(q_mid_b[:, kc], k_mid_b[:, kc])
    cross = [jnp.zeros((C, TS), F32)]
    for c in range(1, nc):
        cross.append(_dot_nt(q_in_b[c * C:(c + 1) * C, kc], k_cross[c][:, kc]))
    a = jnp.where(same_chunk_causal, s_diag, jnp.concatenate(cross, axis=0)).astype(BF16)
    st = state_ref[hh]
    o = _dot(a, v_h) + _dot(q_s0[:, kc], st.astype(BF16))
    d_col = jnp.broadcast_to(decay_end[:, kc], (DK, DK)).T
    state_ref[hh] = st * jnp.concatenate([d_col] * (DV // DK), axis=1) + _dot_tn(k_end[:, kc], v_h)
    rms = lax.rsqrt(jnp.mean(o * o, axis=-1, keepdims=True) + RMS_EPS)
    r_h = ra[:, vc]
    ya_ref[:, vc] = (o * rms) * gnorm * (r_h * jax.nn.sigmoid(r_h))


def _swa_project(h, pos, r0, invf_ref, wqbt_ref, wkvt_ref, kb_ref, vb_ref):
    TS = h.shape[0]
    W = SWA_WINDOW
    HD = SWA_HD
    n_heads = wqbt_ref.shape[0] // HD
    half = ROPE_DIM // 2
    ang = invf_ref[...] * pos.astype(F32)
    cos_t, sin_t = jnp.cos(ang), jnp.sin(ang)

    def rope(t, heads):
        t3 = t.reshape(heads, HD, TS)
        t1, t2 = t3[:, 0:half, :], t3[:, half:ROPE_DIM, :]
        return jnp.concatenate([t1 * cos_t - t2 * sin_t, t2 * cos_t + t1 * sin_t, t3[:, ROPE_DIM:, :]], axis=1)

    q3 = rope(_dot_nt(wqbt_ref[...], h), n_heads).astype(BF16)
    kvt = _dot_nt(wkvt_ref[...], h)
    n_kv = SWA_KV_HEADS * HD
    k_rot = rope(kvt[0:n_kv], SWA_KV_HEADS).reshape(n_kv, TS)
    kb_ref[W + r0:W + r0 + TS, :] = k_rot.T.astype(BF16)
    vb_ref[:, W + r0:W + r0 + TS] = kvt[n_kv:2 * n_kv].astype(BF16)
    return q3


def _swa_scores(j, jl, g, q3, kb_ref):
    W = SWA_WINDOW
    per_kv = q3.shape[0] // SWA_KV_HEADS
    qcols = slice(jl * W, (jl + 1) * W)
    k_band = kb_ref[j * W:j * W + 2 * W, :]
    q_g = jnp.concatenate([q3[g * per_kv + e, :, qcols] for e in range(per_kv)], axis=1)
    zeros_q = jnp.zeros_like(q_g)
    q_pad = jnp.concatenate([zeros_q] * g + [q_g] + [zeros_q] * (SWA_KV_HEADS - 1 - g), axis=0)
    return _dot(k_band, q_pad)


def _swa_softmax_pv(j, jl, g, sc_all, s_idx, sinks_ref, vb_ref, ybt_ref, between_heads):
    W = SWA_WINDOW
    HD = SWA_HD
    per_kv = sc_all.shape[1] // W
    kj = lax.broadcasted_iota(jnp.int32, (W, W), 0)
    qi = lax.broadcasted_iota(jnp.int32, (W, W), 1)
    from_prev = kj > qi
    p_parts, inv_parts = [], []
    for e in range(per_kv):
        sink = sinks_ref[g * per_kv + e] * LOG2E
        s_prev = sc_all[0:W, e * W:(e + 1) * W]
        if j == 0:
            s_prev = jnp.where(s_idx > 0, s_prev, -jnp.inf)
        sc = jnp.where(from_prev, s_prev, sc_all[W:2 * W, e * W:(e + 1) * W])
        m = jnp.maximum(jnp.max(sc, axis=0, keepdims=True), sink)
        pexp = jnp.exp2(sc - m)
        denom = jnp.sum(pexp, axis=0, keepdims=True) + jnp.exp2(sink - m)
        p_parts.append(jnp.concatenate([jnp.where(from_prev, pexp, 0.0), jnp.where(from_prev, 0.0, pexp)],
                                       axis=0).astype(BF16))
        inv_parts.append(1.0 / denom)
        between_heads(e)
    v_band = vb_ref[g * HD:(g + 1) * HD, j * W:j * W + 2 * W]
    o_t = _dot(v_band, jnp.concatenate(p_parts, axis=1)) * jnp.concatenate(inv_parts, axis=1)
    for e in range(per_kv):
        hd = g * per_kv + e
        ybt_ref[hd * HD:(hd + 1) * HD, jl * W:(jl + 1) * W] = o_t[:, e * W:(e + 1) * W]


def _mixer_kernel(x_ref, pos_ref, mod_ref, invf_ref, sinks_ref, tri3_ref,
                  wtok_ref, wlra_ref, wg_ref, wfeat_ref, wlr_ref, blr_ref, gnorm_ref, wo_ref, ln1g_ref, ln1b_ref,
                  o_ref,
                  state_ref, kb_ref, vb_ref, ya_ref, ybt_ref, qk_ref, va_ref, ra_ref, gates_ref,
                  *, alpha, tok_widths, feat_heights):
    TS, D = x_ref.shape[1], x_ref.shape[2]
    W = SWA_WINDOW
    s_idx = pl.program_id(1)

    tok_offs = [sum(tok_widths[:i]) for i in range(len(tok_widths) + 1)]
    wqa_ref, wka_ref, wva_ref, wra_ref = (
        wtok_ref.at[:, tok_offs[i]:tok_offs[i + 1]] for i in range(len(tok_widths)))
    feat_offs = [sum(feat_heights[:i]) for i in range(len(feat_heights) + 1)]
    wqbt_ref, wkvt_ref = (wfeat_ref.at[feat_offs[i]:feat_offs[i + 1], :] for i in range(len(feat_heights)))

    @pl.when(s_idx == 0)
    def _():
        state_ref[...] = jnp.zeros_like(state_ref)
        kb_ref[0:W, :] = jnp.zeros((W, kb_ref.shape[1]), BF16)
        vb_ref[:, 0:W] = jnp.zeros((vb_ref.shape[0], W), BF16)

    refs = (pos_ref, mod_ref, invf_ref, sinks_ref, tri3_ref,
            wqa_ref, wka_ref, wva_ref, wra_ref, wlra_ref, wlr_ref, blr_ref, gnorm_ref,
            wqbt_ref, wkvt_ref, wg_ref, wo_ref, ln1g_ref, ln1b_ref, state_ref, kb_ref, vb_ref)
    for sub in range(TS // SUB_TILE):
        scratch = tuple(r.at[sub] for r in (ya_ref, ybt_ref, qk_ref, va_ref, ra_ref, gates_ref))
        _mixer_subtile(sub, s_idx, x_ref, o_ref, refs, scratch, alpha)
    kb_ref[0:W, :] = kb_ref[TS:TS + W, :]
    vb_ref[:, 0:W] = vb_ref[:, TS:TS + W]


def _mixer_subtile(sub, s_idx, x_ref, o_ref, refs, scratch, alpha):
    (pos_ref, mod_ref, invf_ref, sinks_ref, tri3_ref,
     wqa_ref, wka_ref, wva_ref, wra_ref, wlra_ref, wlr_ref, blr_ref, gnorm_ref,
     wqbt_ref, wkvt_ref, wg_ref, wo_ref, ln1g_ref, ln1b_ref, state_ref, kb_ref, vb_ref) = refs
    ya_ref, ybt_ref, qk_ref, va_ref, ra_ref, gates_ref = scratch
    TS = SUB_TILE
    D = x_ref.shape[2]
    W = SWA_WINDOW
    HD = wqa_ref.shape[1]
    DK = HD // GLA_HEADS
    r0 = sub * TS
    rows = slice(r0, r0 + TS)

    x = x_ref[0, rows, :]
    mod = mod_ref[0]
    shift1, scale1, gate1 = mod[0:1], mod[1:2], mod[2:3]
    h = (_layer_norm(x) * (1.0 + scale1) + shift1).astype(BF16)

    def chunks(dst_ref, col0, w_ref, post):
        def make(c0):
            def run():
                dst_ref[:, col0 + c0:col0 + c0 + MXU_COLS] = post(_dot(h, w_ref[:, c0:c0 + MXU_COLS]))
            return run
        return [make(c0) for c0 in range(0, w_ref.shape[1], MXU_COLS)]

    pending = (chunks(qk_ref, 0, wqa_ref, lambda v: v * (DK ** -0.5)) + chunks(qk_ref, HD, wka_ref, lambda v: v)
               + chunks(va_ref, 0, wva_ref, lambda v: v.astype(BF16)) + chunks(ra_ref, 0, wra_ref, lambda v: v)
               + chunks(gates_ref, 0, wg_ref, lambda v: v))
    n_qk = 2 * HD // MXU_COLS
    n_gla = n_qk + 2 * wva_ref.shape[1] // MXU_COLS
    emitted = [0]

    def emit_upto(n):
        while emitted[0] < min(n, len(pending)):
            pending[emitted[0]]()
            emitted[0] += 1

    log_a = _gla_gates(h, wlra_ref, wlr_ref, blr_ref)
    q3 = _swa_project(h, pos_ref[0, :, rows], r0, invf_ref, wqbt_ref, wkvt_ref, kb_ref, vb_ref)
    emit_upto(n_qk)
    gcum = _dot(tri3_ref[...], jnp.concatenate(_split3_bf16(log_a), axis=0))
    gnorm = gnorm_ref[...]
    j0 = r0 // W
    blocks = [(j0 + jl, jl, g) for jl in range(TS // W) for g in range(SWA_KV_HEADS)]
    nb = len(blocks)
    per_kv = q3.shape[0] // SWA_KV_HEADS
    head_blocks = blocks[nb // 2:]
    heads_per_block = -(-GLA_HEADS // len(head_blocks))
    gla_next = [0]
    sc = _swa_scores(*blocks[0], q3, kb_ref)
    factors = None
    for bi, (j, jl, g) in enumerate(blocks):
        sc_next = _swa_scores(*blocks[bi + 1], q3, kb_ref) if bi + 1 < nb else None

        def between_heads(e, bi=bi):
            if e % 2 == 0:
                emit_upto(emitted[0] + 1)
            elif bi >= nb // 2 and (e + 1) % (per_kv // heads_per_block) == 0 and gla_next[0] < GLA_HEADS:
                emit_upto(n_gla)
                _gla_head(gla_next[0], factors, va_ref, ra_ref, gnorm, state_ref, ya_ref)
                gla_next[0] += 1

        _swa_softmax_pv(j, jl, g, sc, s_idx, sinks_ref, vb_ref, ybt_ref, between_heads)
        sc = sc_next
        if bi == 0:
            factors = _gla_factors(gcum, qk_ref[:, 0:HD], qk_ref[:, HD:2 * HD])
    emit_upto(len(pending))
    while gla_next[0] < GLA_HEADS:
        _gla_head(gla_next[0], factors, va_ref, ra_ref, gnorm, state_ref, ya_ref)
        gla_next[0] += 1

    y = (jax.nn.sigmoid(gates_ref[:, 0:D]) * ya_ref[...]
         + jax.nn.sigmoid(gates_ref[:, D:2 * D]) * ybt_ref[...].T)
    t = _dot(y.astype(BF16), wo_ref[...])
    o_ref[0, rows, :] = _layer_norm(alpha * x + gate1 * t) * ln1g_ref[...] + ln1b_ref[...]


def _const_spec(shape):
    nd = len(shape)
    return pl.BlockSpec(shape, lambda b, s: (0,) * nd)


def _token_mixer(x, positions, mod, weights, *, alpha):
    B, S, D = x.shape
    TS, ST = SEQ_TILE, SUB_TILE
    NS = TS // ST
    (invf, sinks, tri3, wtok, wlra, wg, wfeat, wlr, blr, gnorm, wo, ln1g, ln1b, tok_widths, feat_heights) = weights
    vmem_consts = (tri3, wtok, wlra, wg, wfeat, wlr, blr, gnorm, wo, ln1g, ln1b)
    n_qa, n_ka, n_va, n_ra = tok_widths
    n_g = wg.shape[1]
    DK = n_qa // GLA_HEADS
    DV = n_va // GLA_HEADS
    n_kv = SWA_KV_HEADS * SWA_HD
    in_specs = [
        pl.BlockSpec((1, TS, D), lambda b, s: (b, s, 0)),
        pl.BlockSpec((1, 1, TS), lambda b, s: (b, 0, s)),
        pl.BlockSpec((1, N_MOD, D), lambda b, s: (b, 0, 0)),
        _const_spec(invf.shape),
        pl.BlockSpec(memory_space=pltpu.SMEM),
    ] + [_const_spec(w.shape) for w in vmem_consts]
    return pl.pallas_call(
        functools.partial(_mixer_kernel, alpha=alpha, tok_widths=tok_widths, feat_heights=feat_heights),
        grid=(B, S // TS),
        in_specs=in_specs,
        out_specs=pl.BlockSpec((1, TS, D), lambda b, s: (b, s, 0)),
        out_shape=jax.ShapeDtypeStruct((B, S, D), F32),
        scratch_shapes=[
            pltpu.VMEM((GLA_HEADS, DK, DV), F32),
            pltpu.VMEM((SWA_WINDOW + TS, n_kv), BF16),
            pltpu.VMEM((n_kv, SWA_WINDOW + TS), BF16),
            pltpu.VMEM((NS, ST, D), F32),
            pltpu.VMEM((NS, feat_heights[0], ST), F32),
            pltpu.VMEM((NS, ST, n_qa + n_ka), F32),
            pltpu.VMEM((NS, ST, n_va), BF16),
            pltpu.VMEM((NS, ST, n_ra), F32),
            pltpu.VMEM((NS, ST, n_g), F32),
        ],
        compiler_params=pltpu.CompilerParams(
            dimension_semantics=("arbitrary", "arbitrary"), vmem_limit_bytes=VMEM_LIMIT_BYTES),
        name="token_mixer",
    )(x, positions.reshape(B, 1, S), mod, invf, sinks, *vmem_consts)


def _ffn_kernel(x_ref, mod_ref, wup_ref, cw_ref, cb_ref, wdn_ref, g_ref, b_ref, o_ref, u_ref, *, alpha):
    TS = x_ref.shape[1]
    F = wdn_ref.shape[0]
    PAD = SUBLANES
    s_idx = pl.program_id(1)

    @pl.when(s_idx == 0)
    def _():
        u_ref[0:PAD, :] = jnp.zeros((PAD, u_ref.shape[1]), F32)

    mod = mod_ref[0]
    shift2, scale2, gate2 = mod[3:4], mod[4:5], mod[5:6]
    x = x_ref[0]
    h2 = (_layer_norm(x) * (1.0 + scale2) + shift2).astype(BF16)
    u_ref[PAD:PAD + TS, :] = _dot(h2, wup_ref[...])
    cw = cw_ref[...]
    u = cb_ref[...]
    for t in range(CONV_W):
        off = PAD - (CONV_W - 1) + t
        u = u + cw[t:t + 1] * u_ref[off:off + TS, :]
    u_ref[0:PAD, :] = u_ref[TS:TS + PAD, :]
    u_gate, u_val = u[:, 0:F], u[:, F:2 * F]
    f = 0.5 * u_gate * (1.0 + lax.erf(u_gate * (2.0 ** -0.5))) * u_val
    t_out = _dot(f.astype(BF16), wdn_ref[...])
    o_ref[0] = _layer_norm(alpha * x + gate2 * t_out) * g_ref[...] + b_ref[...]


def _channel_mixer(x, mod, wup, cw, cb, wdn, g, b, *, alpha):
    B, S, D = x.shape
    TS = SEQ_TILE
    consts = (wup, cw, cb, wdn, g, b)
    return pl.pallas_call(
        functools.partial(_ffn_kernel, alpha=alpha),
        grid=(B, S // TS),
        in_specs=[
            pl.BlockSpec((1, TS, D), lambda b_, s: (b_, s, 0)),
            pl.BlockSpec((1, N_MOD, D), lambda b_, s: (b_, 0, 0)),
        ] + [_const_spec(w.shape) for w in consts],
        out_specs=pl.BlockSpec((1, TS, D), lambda b_, s: (b_, s, 0)),
        out_shape=jax.ShapeDtypeStruct((B, S, D), F32),
        scratch_shapes=[pltpu.VMEM((SUBLANES + TS, wup.shape[1]), F32)],
        compiler_params=pltpu.CompilerParams(
            dimension_semantics=("arbitrary", "arbitrary"), vmem_limit_bytes=VMEM_LIMIT_BYTES),
        name="channel_mixer",
    )(x, mod, *consts)


def kernel(x, c, positions, w_ada, b_ada, w_in, gla_w_lr, gla_b_lr, gla_norm_g, swa_sinks,
           w_o, ln1_g, ln1_b, w_up, conv_w, conv_b, w_down, ln2_g, ln2_b):
    B, S, D = x.shape
    depth = w_ada.shape[0]
    alpha = (2.0 * depth) ** 0.25
    assert S % SEQ_TILE == 0 and SUB_TILE % SWA_WINDOW == 0 and SUB_TILE % GLA_CHUNK == 0
    assert D % LANES == 0 and (N_MOD * D) % MOD_COL_TILE == 0

    dk_tot = gla_w_lr.shape[2]
    dv_tot = D
    n_q = D
    n_kv = SWA_KV_HEADS * SWA_HD
    assert n_kv == LANES
    splits = (dk_tot, dk_tot, dv_tot, dv_tot, GLA_RANK, n_q, n_kv, n_kv, 2 * D)
    offs = [0]
    for n in splits:
        offs.append(offs[-1] + n)
    assert offs[-1] == w_in.shape[2]

    inv_freq = ROPE_THETA ** (-(jnp.arange(0, ROPE_DIM, 2, dtype=F32) / ROPE_DIM))
    invf = inv_freq.reshape(ROPE_DIM // 2, 1)
    tok = jnp.arange(SUB_TILE)
    tri = (tok[:, None] // GLA_CHUNK == tok[None, :] // GLA_CHUNK) & (tok[None, :] <= tok[:, None])
    tri3 = jnp.tile(tri.astype(BF16), (1, 3))

    for l in range(depth):
        mod = _modulation(c, w_ada[l], b_ada[l]).reshape(B, N_MOD, D)
        wl = w_in[l]
        col = lambda i: wl[:, offs[i]:offs[i + 1]]
        wtok = wl[:, offs[0]:offs[4]].astype(BF16)
        tok_widths = (dk_tot, dk_tot, dv_tot, dv_tot)
        wlra = jnp.pad(col(4), ((0, 0), (0, LANES - GLA_RANK))).astype(BF16)
        wg = col(8).astype(BF16)
        q_scale = jnp.concatenate([jnp.full((n_q,), SWA_HD ** -0.5 * LOG2E, F32), jnp.ones((2 * n_kv,), F32)])
        wfeat = (wl[:, offs[5]:offs[8]] * q_scale).T.astype(BF16)
        feat_heights = (n_q, 2 * n_kv)
        wlr = jnp.pad(gla_w_lr[l].astype(BF16), ((0, LANES - GLA_RANK), (0, 0)))
        weights = (invf, swa_sinks[l], tri3, wtok, wlra, wg, wfeat, wlr,
                   gla_b_lr[l].reshape(1, -1), gla_norm_g[l].reshape(1, -1), w_o[l].astype(BF16),
                   ln1_g[l].reshape(1, D), ln1_b[l].reshape(1, D), tok_widths, feat_heights)
        x = _token_mixer(x, positions, mod, weights, alpha=alpha)
        x = _channel_mixer(x, mod, w_up[l].astype(BF16), conv_w[l], conv_b[l].reshape(1, -1),
                           w_down[l].astype(BF16), ln2_g[l].reshape(1, D), ln2_b[l].reshape(1, D),
                           alpha=alpha)
    return x
```

```python
import functools

import jax
import jax.numpy as jnp
from jax import lax
from jax.experimental import pallas as pl
from jax.experimental.pallas import tpu as pltpu

LANES = 128
SUBLANES = 8
MXU_COLS = 256
VMEM_LIMIT_BYTES = 56 * 1024 * 1024

GLA_HEADS = 4
GLA_RANK = 16
GLA_TAU = 16.0
GLA_CHUNK = 64
SWA_HD = 64
SWA_KV_HEADS = 2
SWA_WINDOW = 128
ROPE_DIM = SWA_HD // 4
ROPE_THETA = 500000.0
CONV_W = 3
LN_EPS = 1e-5
RMS_EPS = 1e-6
N_MOD = 6
LOG2E = 1.4426950408889634

SUB_TILE = 256
MIXER_TILE = 4 * SUB_TILE
FFN_TILE = 2 * SUB_TILE
SCRATCH_SETS = 2
MOD_COL_TILE = 1536

BF16 = jnp.bfloat16
F32 = jnp.float32


def _dot(a, b):
    return jnp.dot(a, b, preferred_element_type=F32)


def _dot_nt(a, b):
    return lax.dot_general(a, b, (((1,), (1,)), ((), ())), preferred_element_type=F32)


def _dot_tn(a, b):
    return lax.dot_general(a, b, (((0,), (0,)), ((), ())), preferred_element_type=F32)


def _layer_norm(x):
    mu = jnp.mean(x, axis=-1, keepdims=True)
    xc = x - mu
    var = jnp.mean(xc * xc, axis=-1, keepdims=True)
    return xc * lax.rsqrt(var + LN_EPS)


def _split3_bf16(a):
    hi = a.astype(BF16)
    r1 = a - hi.astype(F32)
    mid = r1.astype(BF16)
    lo = (r1 - mid.astype(F32)).astype(BF16)
    return hi, mid, lo


def _mod_kernel(c_ref, w_ref, b_ref, o_ref):
    c = c_ref[...]
    c_act = c * jax.nn.sigmoid(c)
    o_ref[...] = _dot(c_act.astype(BF16), w_ref[...].astype(BF16)) + b_ref[...]


def _modulation(c, w_ada, b_ada):
    B, D = c.shape
    N = w_ada.shape[1]
    return pl.pallas_call(
        _mod_kernel,
        grid=(N // MOD_COL_TILE,),
        in_specs=[
            pl.BlockSpec((B, D), lambda j: (0, 0)),
            pl.BlockSpec((D, MOD_COL_TILE), lambda j: (0, j)),
            pl.BlockSpec((1, MOD_COL_TILE), lambda j: (0, j)),
        ],
        out_specs=pl.BlockSpec((B, MOD_COL_TILE), lambda j: (0, j)),
        out_shape=jax.ShapeDtypeStruct((B, N), F32),
        compiler_params=pltpu.CompilerParams(
            dimension_semantics=("arbitrary",), vmem_limit_bytes=VMEM_LIMIT_BYTES),
        name="adaln_mod",
    )(c, w_ada, b_ada.reshape(1, N))


def _gla_gates(h, wlra_ref, wlr_ref, blr_ref):
    lra = _dot(h, wlra_ref[...]).astype(BF16)
    z = _dot(lra, wlr_ref[...]) + blr_ref[...]
    return (jnp.minimum(z, 0.0) - jnp.log1p(jnp.exp(-jnp.abs(z)))) * (1.0 / GLA_TAU)


def _gla_factors(gcum, qa, ka):
    TS, HD = gcum.shape
    C = GLA_CHUNK
    nc = TS // C
    g3 = gcum.reshape(nc, C, HD)
    g_mid = g3[:, C // 2 - 1:C // 2, :]
    g_last = g3[:, C - 1:C, :]
    q_mid = qa.reshape(nc, C, HD) * jnp.exp(g3 - g_mid)
    k_mid = ka.reshape(nc, C, HD) * jnp.exp(g_mid - g3)
    q_in = q_mid * jnp.exp(g_mid)
    k_out = k_mid * jnp.exp(g_last - g_mid)
    prefix = [jnp.zeros((1, HD), F32)]
    for c in range(nc):
        prefix.append(prefix[-1] + g_last[c])
    q_mid_b = q_mid.reshape(TS, HD).astype(BF16)
    k_mid_b = k_mid.reshape(TS, HD).astype(BF16)
    q_in_b = q_in.reshape(TS, HD).astype(BF16)
    q_s0 = jnp.concatenate([q_in[c] * jnp.exp(prefix[c]) for c in range(nc)], axis=0).astype(BF16)
    k_end = jnp.concatenate([k_out[c] * jnp.exp(prefix[nc] - prefix[c + 1]) for c in range(nc)],
                            axis=0).astype(BF16)
    k_cross = [None]
    for c in range(1, nc):
        parts = [k_out[cp] * jnp.exp(prefix[c] - prefix[cp + 1]) for cp in range(c)]
        parts.append(jnp.zeros((TS - c * C, HD), F32))
        k_cross.append(jnp.concatenate(parts, axis=0).astype(BF16))
    decay_end = jnp.exp(prefix[nc])
    return q_mid_b, k_mid_b, q_in_b, q_s0, k_end, k_cross, decay_end


def _gla_head(hh, factors, va, ra, gnorm, state_ref, ya_ref):
    q_mid_b, k_mid_b, q_in_b, q_s0, k_end, k_cross, decay_end = factors
    TS, HD = q_mid_b.shape
    C = GLA_CHUNK
    nc = TS // C
    DK = HD // GLA_HEADS
    DV = va.shape[1] // GLA_HEADS
    kc = slice(hh * DK, (hh + 1) * DK)
    vc = slice(hh * DV, (hh + 1) * DV)
    ri = lax.broadcasted_iota(jnp.int32, (TS, TS), 0)
    ci = lax.broadcasted_iota(jnp.int32, (TS, TS), 1)
    same_chunk_causal = (ri // C == ci // C) & (ci <= ri)
    v_h = va[:, vc]
    s_diag = _dot_nt(q_mid_b[:, kc], k_mid_b[:, kc])
    cross = [jnp.zeros((C, TS), F32)]
    for c in range(1, nc):
        cross.append(_dot_nt(q_in_b[c * C:(c + 1) * C, kc], k_cross[c][:, kc]))
    a = jnp.where(same_chunk_causal, s_diag, jnp.concatenate(cross, axis=0)).astype(BF16)
    st = state_ref[hh]
    o = _dot(a, v_h) + _dot(q_s0[:, kc], st.astype(BF16))
    d_col = jnp.broadcast_to(decay_end[:, kc], (DK, DK)).T
    state_ref[hh] = st * jnp.concatenate([d_col] * (DV // DK), axis=1) + _dot_tn(k_end[:, kc], v_h)
    rms = lax.rsqrt(jnp.mean(o * o, axis=-1, keepdims=True) + RMS_EPS)
    r_h = ra[:, vc]
    ya_ref[:, vc] = (o * rms) * gnorm * (r_h * jax.nn.sigmoid(r_h))


def _swa_project(h, pos, r0, invf_ref, wqbt_ref, wkvt_ref, kb_ref, vb_ref):
    TS = h.shape[0]
    W = SWA_WINDOW
    HD = SWA_HD
    n_heads = wqbt_ref.shape[0] // HD
    half = ROPE_DIM // 2
    ang = invf_ref[...] * pos.astype(F32)
    cos_t, sin_t = jnp.cos(ang), jnp.sin(ang)

    def rope(t, heads):
        t3 = t.reshape(heads, HD, TS)
        t1, t2 = t3[:, 0:half, :], t3[:, half:ROPE_DIM, :]
        return jnp.concatenate([t1 * cos_t - t2 * sin_t, t2 * cos_t + t1 * sin_t, t3[:, ROPE_DIM:, :]], axis=1)

    q3 = rope(_dot_nt(wqbt_ref[...], h), n_heads).astype(BF16)
    kvt = _dot_nt(wkvt_ref[...], h)
    n_kv = SWA_KV_HEADS * HD
    k_rot = rope(kvt[0:n_kv], SWA_KV_HEADS).reshape(n_kv, TS)
    kb_ref[W + r0:W + r0 + TS, :] = k_rot.T.astype(BF16)
    vb_ref[:, W + r0:W + r0 + TS] = kvt[n_kv:2 * n_kv].astype(BF16)
    return q3


def _swa_scores(j, jl, g, q3, kb_ref):
    W = SWA_WINDOW
    per_kv = q3.shape[0] // SWA_KV_HEADS
    qcols = slice(jl * W, (jl + 1) * W)
    k_band = kb_ref[j * W:j * W + 2 * W, :]
    q_g = jnp.concatenate([q3[g * per_kv + e, :, qcols] for e in range(per_kv)], axis=1)
    zeros_q = jnp.zeros_like(q_g)
    q_pad = jnp.concatenate([zeros_q] * g + [q_g] + [zeros_q] * (SWA_KV_HEADS - 1 - g), axis=0)
    return _dot(k_band, q_pad)


def _swa_softmax_pv(j, jl, g, sc_all, s_idx, sinks_ref, vb_ref, ybt_ref, between_heads):
    W = SWA_WINDOW
    HD = SWA_HD
    per_kv = sc_all.shape[1] // W
    kj = lax.broadcasted_iota(jnp.int32, (W, W), 0)
    qi = lax.broadcasted_iota(jnp.int32, (W, W), 1)
    from_prev = kj > qi
    p_parts, inv_parts = [], []
    for e in range(per_kv):
        sink = sinks_ref[g * per_kv + e] * LOG2E
        s_prev = sc_all[0:W, e * W:(e + 1) * W]
        if j == 0:
            s_prev = jnp.where(s_idx > 0, s_prev, -jnp.inf)
        sc = jnp.where(from_prev, s_prev, sc_all[W:2 * W, e * W:(e + 1) * W])
        m = jnp.maximum(jnp.max(sc, axis=0, keepdims=True), sink)
        pexp = jnp.exp2(sc - m)
        denom = jnp.sum(pexp, axis=0, keepdims=True) + jnp.exp2(sink - m)
        p_parts.append(jnp.concatenate([jnp.where(from_prev, pexp, 0.0), jnp.where(from_prev, 0.0, pexp)],
                                       axis=0).astype(BF16))
        inv_parts.append(1.0 / denom)
        between_heads(e)
    v_band = vb_ref[g * HD:(g + 1) * HD, j * W:j * W + 2 * W]
    o_t = _dot(v_band, jnp.concatenate(p_parts, axis=1)) * jnp.concatenate(inv_parts, axis=1)
    for e in range(per_kv):
        hd = g * per_kv + e
        ybt_ref[hd * HD:(hd + 1) * HD, jl * W:(jl + 1) * W] = o_t[:, e * W:(e + 1) * W]


def _mixer_kernel(x_ref, pos_ref, mod_ref, invf_ref, sinks_ref, tri3_ref,
                  wtok_ref, wlra_ref, wg_ref, wfeat_ref, wlr_ref, blr_ref, gnorm_ref, wo_ref, ln1g_ref, ln1b_ref,
                  o_ref,
                  state_ref, kb_ref, vb_ref, ya_ref, ybt_ref, qk_ref, va_ref, ra_ref, gates_ref,
                  *, alpha, tok_widths, feat_heights):
    TS, D = x_ref.shape[1], x_ref.shape[2]
    W = SWA_WINDOW
    s_idx = pl.program_id(1)

    tok_offs = [sum(tok_widths[:i]) for i in range(len(tok_widths) + 1)]
    wqa_ref, wka_ref, wva_ref, wra_ref = (
        wtok_ref.at[:, tok_offs[i]:tok_offs[i + 1]] for i in range(len(tok_widths)))
    feat_offs = [sum(feat_heights[:i]) for i in range(len(feat_heights) + 1)]
    wqbt_ref, wkvt_ref = (wfeat_ref.at[feat_offs[i]:feat_offs[i + 1], :] for i in range(len(feat_heights)))

    @pl.when(s_idx == 0)
    def _():
        state_ref[...] = jnp.zeros_like(state_ref)
        kb_ref[0:W, :] = jnp.zeros((W, kb_ref.shape[1]), BF16)
        vb_ref[:, 0:W] = jnp.zeros((vb_ref.shape[0], W), BF16)

    refs = (pos_ref, mod_ref, invf_ref, sinks_ref, tri3_ref,
            wqa_ref, wka_ref, wva_ref, wra_ref, wlra_ref, wlr_ref, blr_ref, gnorm_ref,
            wqbt_ref, wkvt_ref, wg_ref, wo_ref, ln1g_ref, ln1b_ref, state_ref, kb_ref, vb_ref)
    n_sub = TS // SUB_TILE
    phases = []
    for sub in range(n_sub):
        scratch = tuple(r.at[sub % SCRATCH_SETS] for r in (ya_ref, ybt_ref, qk_ref, va_ref, ra_ref, gates_ref))
        phases.append(_mixer_subtile(sub, s_idx, x_ref, o_ref, refs, scratch, alpha))
    phases[0][0]()
    for sub in range(n_sub):
        head, body, tail = phases[sub]
        body()
        if sub + 1 < n_sub:
            phases[sub + 1][0]()
        tail()
    kb_ref[0:W, :] = kb_ref[TS:TS + W, :]
    vb_ref[:, 0:W] = vb_ref[:, TS:TS + W]


def _mixer_subtile(sub, s_idx, x_ref, o_ref, refs, scratch, alpha):
    (pos_ref, mod_ref, invf_ref, sinks_ref, tri3_ref,
     wqa_ref, wka_ref, wva_ref, wra_ref, wlra_ref, wlr_ref, blr_ref, gnorm_ref,
     wqbt_ref, wkvt_ref, wg_ref, wo_ref, ln1g_ref, ln1b_ref, state_ref, kb_ref, vb_ref) = refs
    ya_ref, ybt_ref, qk_ref, va_ref, ra_ref, gates_ref = scratch
    TS = SUB_TILE
    D = x_ref.shape[2]
    W = SWA_WINDOW
    HD = wqa_ref.shape[1]
    DK = HD // GLA_HEADS
    r0 = sub * TS
    rows = slice(r0, r0 + TS)

    mod = mod_ref[0]
    shift1, scale1, gate1 = mod[0:1], mod[1:2], mod[2:3]
    n_qk = 2 * HD // MXU_COLS
    n_gla = n_qk + 2 * wva_ref.shape[1] // MXU_COLS
    st = {"emitted": 0}

    def emit_upto(n):
        while st["emitted"] < min(n, len(st["pending"])):
            st["pending"][st["emitted"]]()
            st["emitted"] += 1

    def head():
        x = x_ref[0, rows, :]
        h = (_layer_norm(x) * (1.0 + scale1) + shift1).astype(BF16)

        def chunks(dst_ref, col0, w_ref, post):
            def make(c0):
                def run():
                    dst_ref[:, col0 + c0:col0 + c0 + MXU_COLS] = post(_dot(h, w_ref[:, c0:c0 + MXU_COLS]))
                return run
            return [make(c0) for c0 in range(0, w_ref.shape[1], MXU_COLS)]

        st["pending"] = (chunks(qk_ref, 0, wqa_ref, lambda v: v * (DK ** -0.5))
                         + chunks(qk_ref, HD, wka_ref, lambda v: v)
                         + chunks(va_ref, 0, wva_ref, lambda v: v.astype(BF16))
                         + chunks(ra_ref, 0, wra_ref, lambda v: v)
                         + chunks(gates_ref, 0, wg_ref, lambda v: v))
        st["x"] = x
        st["log_a"] = _gla_gates(h, wlra_ref, wlr_ref, blr_ref)
        st["q3"] = _swa_project(h, pos_ref[0, :, rows], r0, invf_ref, wqbt_ref, wkvt_ref, kb_ref, vb_ref)
        emit_upto(n_qk)

    def body():
        _mixer_body(st, emit_upto, n_gla, r0, s_idx, refs, scratch)

    def tail():
        y = (jax.nn.sigmoid(gates_ref[:, 0:D]) * ya_ref[...]
             + jax.nn.sigmoid(gates_ref[:, D:2 * D]) * ybt_ref[...].T)
        t = _dot(y.astype(BF16), wo_ref[...])
        o_ref[0, rows, :] = _layer_norm(alpha * st["x"] + gate1 * t) * ln1g_ref[...] + ln1b_ref[...]

    return head, body, tail


def _mixer_body(st, emit_upto, n_gla, r0, s_idx, refs, scratch):
    (pos_ref, mod_ref, invf_ref, sinks_ref, tri3_ref,
     wqa_ref, wka_ref, wva_ref, wra_ref, wlra_ref, wlr_ref, blr_ref, gnorm_ref,
     wqbt_ref, wkvt_ref, wg_ref, wo_ref, ln1g_ref, ln1b_ref, state_ref, kb_ref, vb_ref) = refs
    ya_ref, ybt_ref, qk_ref, va_ref, ra_ref, gates_ref = scratch
    TS = SUB_TILE
    W = SWA_WINDOW
    HD = wqa_ref.shape[1]
    log_a, q3 = st["log_a"], st["q3"]
    gcum = _dot(tri3_ref[...], jnp.concatenate(_split3_bf16(log_a), axis=0))
    gnorm = gnorm_ref[...]
    j0 = r0 // W
    blocks = [(j0 + jl, jl, g) for jl in range(TS // W) for g in range(SWA_KV_HEADS)]
    nb = len(blocks)
    per_kv = q3.shape[0] // SWA_KV_HEADS
    head_blocks = blocks[nb // 2:]
    heads_per_block = -(-GLA_HEADS // len(head_blocks))
    gla_next = [0]
    sc = _swa_scores(*blocks[0], q3, kb_ref)
    factors = None
    for bi, (j, jl, g) in enumerate(blocks):
        sc_next = _swa_scores(*blocks[bi + 1], q3, kb_ref) if bi + 1 < nb else None

        def between_heads(e, bi=bi):
            if e % 2 == 0:
                emit_upto(st["emitted"] + 1)
            elif bi >= nb // 2 and (e + 1) % (per_kv // heads_per_block) == 0 and gla_next[0] < GLA_HEADS:
                emit_upto(n_gla)
                _gla_head(gla_next[0], factors, va_ref, ra_ref, gnorm, state_ref, ya_ref)
                gla_next[0] += 1

        _swa_softmax_pv(j, jl, g, sc, s_idx, sinks_ref, vb_ref, ybt_ref, between_heads)
        sc = sc_next
        if bi == 0:
            factors = _gla_factors(gcum, qk_ref[:, 0:HD], qk_ref[:, HD:2 * HD])
    emit_upto(len(st["pending"]))
    while gla_next[0] < GLA_HEADS:
        _gla_head(gla_next[0], factors, va_ref, ra_ref, gnorm, state_ref, ya_ref)
        gla_next[0] += 1


def _const_spec(shape):
    nd = len(shape)
    return pl.BlockSpec(shape, lambda b, s: (0,) * nd)


def _token_mixer(x, positions, mod, weights, *, alpha):
    B, S, D = x.shape
    TS, ST = MIXER_TILE, SUB_TILE
    NS = SCRATCH_SETS
    (invf, sinks, tri3, wtok, wlra, wg, wfeat, wlr, blr, gnorm, wo, ln1g, ln1b, tok_widths, feat_heights) = weights
    vmem_consts = (tri3, wtok, wlra, wg, wfeat, wlr, blr, gnorm, wo, ln1g, ln1b)
    n_qa, n_ka, n_va, n_ra = tok_widths
    n_g = wg.shape[1]
    DK = n_qa // GLA_HEADS
    DV = n_va // GLA_HEADS
    n_kv = SWA_KV_HEADS * SWA_HD
    in_specs = [
        pl.BlockSpec((1, TS, D), lambda b, s: (b, s, 0)),
        pl.BlockSpec((1, 1, TS), lambda b, s: (b, 0, s)),
        pl.BlockSpec((1, N_MOD, D), lambda b, s: (b, 0, 0)),
        _const_spec(invf.shape),
        pl.BlockSpec(memory_space=pltpu.SMEM),
    ] + [_const_spec(w.shape) for w in vmem_consts]
    return pl.pallas_call(
        functools.partial(_mixer_kernel, alpha=alpha, tok_widths=tok_widths, feat_heights=feat_heights),
        grid=(B, S // TS),
        in_specs=in_specs,
        out_specs=pl.BlockSpec((1, TS, D), lambda b, s: (b, s, 0)),
        out_shape=jax.ShapeDtypeStruct((B, S, D), F32),
        scratch_shapes=[
            pltpu.VMEM((GLA_HEADS, DK, DV), F32),
            pltpu.VMEM((SWA_WINDOW + TS, n_kv), BF16),
            pltpu.VMEM((n_kv, SWA_WINDOW + TS), BF16),
            pltpu.VMEM((NS, ST, D), F32),
            pltpu.VMEM((NS, feat_heights[0], ST), F32),
            pltpu.VMEM((NS, ST, n_qa + n_ka), F32),
            pltpu.VMEM((NS, ST, n_va), BF16),
            pltpu.VMEM((NS, ST, n_ra), F32),
            pltpu.VMEM((NS, ST, n_g), F32),
        ],
        compiler_params=pltpu.CompilerParams(
            dimension_semantics=("arbitrary", "arbitrary"), vmem_limit_bytes=VMEM_LIMIT_BYTES),
        name="token_mixer",
    )(x, positions.reshape(B, 1, S), mod, invf, sinks, *vmem_consts)


def _ffn_kernel(x_ref, mod_ref, wup_ref, cw_ref, cb_ref, wdn_ref, g_ref, b_ref, o_ref, u_ref, *, alpha):
    TS = x_ref.shape[1]
    F = wdn_ref.shape[0]
    PAD = SUBLANES
    s_idx = pl.program_id(1)

    @pl.when(s_idx == 0)
    def _():
        u_ref[0:PAD, :] = jnp.zeros((PAD, u_ref.shape[1]), F32)

    mod = mod_ref[0]
    shift2, scale2, gate2 = mod[3:4], mod[4:5], mod[5:6]
    x = x_ref[0]
    h2 = (_layer_norm(x) * (1.0 + scale2) + shift2).astype(BF16)
    u_ref[PAD:PAD + TS, :] = _dot(h2, wup_ref[...])
    cw = cw_ref[...]
    u = cb_ref[...]
    for t in range(CONV_W):
        off = PAD - (CONV_W - 1) + t
        u = u + cw[t:t + 1] * u_ref[off:off + TS, :]
    u_ref[0:PAD, :] = u_ref[TS:TS + PAD, :]
    u_gate, u_val = u[:, 0:F], u[:, F:2 * F]
    f = 0.5 * u_gate * (1.0 + lax.erf(u_gate * (2.0 ** -0.5))) * u_val
    t_out = _dot(f.astype(BF16), wdn_ref[...])
    o_ref[0] = _layer_norm(alpha * x + gate2 * t_out) * g_ref[...] + b_ref[...]


def _channel_mixer(x, mod, wup, cw, cb, wdn, g, b, *, alpha):
    B, S, D = x.shape
    TS = FFN_TILE
    consts = (wup, cw, cb, wdn, g, b)
    return pl.pallas_call(
        functools.partial(_ffn_kernel, alpha=alpha),
        grid=(B, S // TS),
        in_specs=[
            pl.BlockSpec((1, TS, D), lambda b_, s: (b_, s, 0)),
            pl.BlockSpec((1, N_MOD, D), lambda b_, s: (b_, 0, 0)),
        ] + [_const_spec(w.shape) for w in consts],
        out_specs=pl.BlockSpec((1, TS, D), lambda b_, s: (b_, s, 0)),
        out_shape=jax.ShapeDtypeStruct((B, S, D), F32),
        scratch_shapes=[pltpu.VMEM((SUBLANES + TS, wup.shape[1]), F32)],
        compiler_params=pltpu.CompilerParams(
            dimension_semantics=("arbitrary", "arbitrary"), vmem_limit_bytes=VMEM_LIMIT_BYTES),
        name="channel_mixer",
    )(x, mod, *consts)


def kernel(x, c, positions, w_ada, b_ada, w_in, gla_w_lr, gla_b_lr, gla_norm_g, swa_sinks,
           w_o, ln1_g, ln1_b, w_up, conv_w, conv_b, w_down, ln2_g, ln2_b):
    B, S, D = x.shape
    depth = w_ada.shape[0]
    alpha = (2.0 * depth) ** 0.25
    assert S % MIXER_TILE == 0 and S % FFN_TILE == 0 and SUB_TILE % SWA_WINDOW == 0 and SUB_TILE % GLA_CHUNK == 0
    assert D % LANES == 0 and (N_MOD * D) % MOD_COL_TILE == 0

    dk_tot = gla_w_lr.shape[2]
    dv_tot = D
    n_q = D
    n_kv = SWA_KV_HEADS * SWA_HD
    assert n_kv == LANES
    splits = (dk_tot, dk_tot, dv_tot, dv_tot, GLA_RANK, n_q, n_kv, n_kv, 2 * D)
    offs = [0]
    for n in splits:
        offs.append(offs[-1] + n)
    assert offs[-1] == w_in.shape[2]

    inv_freq = ROPE_THETA ** (-(jnp.arange(0, ROPE_DIM, 2, dtype=F32) / ROPE_DIM))
    invf = inv_freq.reshape(ROPE_DIM // 2, 1)
    tok = jnp.arange(SUB_TILE)
    tri = (tok[:, None] // GLA_CHUNK == tok[None, :] // GLA_CHUNK) & (tok[None, :] <= tok[:, None])
    tri3 = jnp.tile(tri.astype(BF16), (1, 3))

    for l in range(depth):
        mod = _modulation(c, w_ada[l], b_ada[l]).reshape(B, N_MOD, D)
        wl = w_in[l]
        col = lambda i: wl[:, offs[i]:offs[i + 1]]
        wtok = wl[:, offs[0]:offs[4]].astype(BF16)
        tok_widths = (dk_tot, dk_tot, dv_tot, dv_tot)
        wlra = jnp.pad(col(4), ((0, 0), (0, LANES - GLA_RANK))).astype(BF16)
        wg = col(8).astype(BF16)
        q_scale = jnp.concatenate([jnp.full((n_q,), SWA_HD ** -0.5 * LOG2E, F32), jnp.ones((2 * n_kv,), F32)])
        wfeat = (wl[:, offs[5]:offs[8]] * q_scale).T.astype(BF16)
        feat_heights = (n_q, 2 * n_kv)
        wlr = jnp.pad(gla_w_lr[l].astype(BF16), ((0, LANES - GLA_RANK), (0, 0)))
        weights = (invf, swa_sinks[l], tri3, wtok, wlra, wg, wfeat, wlr,
                   gla_b_lr[l].reshape(1, -1), gla_norm_g[l].reshape(1, -1), w_o[l].astype(BF16),
                   ln1_g[l].reshape(1, D), ln1_b[l].reshape(1, D), tok_widths, feat_heights)
        x = _token_mixer(x, positions, mod, weights, alpha=alpha)
        x = _channel_mixer(x, mod, w_up[l].astype(BF16), conv_w[l], conv_b[l].reshape(1, -1),
                           w_down[l].astype(BF16), ln2_g[l].reshape(1, D), ln2_b[l].reshape(1, D),
                           alpha=alpha)
    return x
```
